```python
import math
import jax, jax.numpy as jnp
from jax import lax
import numpy as np

D_MODEL = 1024
BATCH = 8
SEQ = 2048
DEPTH = 2
DEC_BATCH = 32
DEC_SEQ = 2048
PAST_LEN = 128

N_DIFF_HEADS = 8
DIFF_HEAD_DIM = D_MODEL // (2 * N_DIFF_HEADS)
DIFF_V_DIM = 2 * DIFF_HEAD_DIM
ROPE_THETA = 10000.0
Q_BLOCK = 128
SGU_CHUNK = 128
SGU_GROUPS = 8
SGU_WIDTH = D_MODEL
SGU_GROUP_DIM = SGU_WIDTH // SGU_GROUPS
D_FF = 2816
N_EXPERTS = 8
TOP_K = 2
D_FF_EXPERT = 2816
LN_EPS = 1e-5
N_A = (DEPTH + 1) // 2
N_B = DEPTH // 2
DEEPNORM_ALPHA = (2 * DEPTH) ** 0.25
DEEPNORM_BETA = (8 * DEPTH) ** -0.25

kernel_name = "hybrid_diffattn_sgu_moe_encoder"


def layer_norm(x, g, b):
    xf = x.astype(jnp.float32)
    mu = jnp.mean(xf, axis=-1, keepdims=True)
    var = jnp.mean(jnp.square(xf - mu), axis=-1, keepdims=True)
    return ((xf - mu) * lax.rsqrt(var + LN_EPS)).astype(x.dtype) * g + b


def rms_norm(x, g):
    xf = x.astype(jnp.float32)
    ms = jnp.mean(jnp.square(xf), axis=-1, keepdims=True)
    return (xf * lax.rsqrt(ms + LN_EPS)).astype(x.dtype) * g


def rope_tables(seq, dtype):
    inv = 1.0 / (ROPE_THETA ** (jnp.arange(0, DIFF_HEAD_DIM, 2, dtype=jnp.float32) / DIFF_HEAD_DIM))
    ang = jnp.arange(seq, dtype=jnp.float32)[:, None] * inv[None, :]
    ang = jnp.concatenate([ang, ang], axis=-1)
    return jnp.cos(ang).astype(dtype), jnp.sin(ang).astype(dtype)


def apply_rope(x, cos, sin):
    x1, x2 = jnp.split(x, 2, axis=-1)
    rot = jnp.concatenate([-x2, x1], axis=-1)
    return x * cos[:, None, None, :] + rot * sin[:, None, None, :]


def adaln_params(c, w_ada, b_ada):
    m = jax.nn.silu(c) @ w_ada + b_ada
    return [p[:, None, :] for p in jnp.split(m, 6, axis=-1)]


def diff_attention(h, w_qkv, lam, subln_g, w_o, lambda_init):
    B, S, _ = h.shape
    H, dh, dv = N_DIFF_HEADS, DIFF_HEAD_DIM, DIFF_V_DIM
    q, k, v = jnp.split(h @ w_qkv, 3, axis=-1)
    q = q.reshape(B, S, H, 2, dh)
    k = k.reshape(B, S, H, 2, dh)
    v = v.reshape(B, S, H, dv)
    cos, sin = rope_tables(S, h.dtype)
    q = apply_rope(q, cos, sin) * (dh ** -0.5)
    k = apply_rope(k, cos, sin)
    lf = lam.astype(jnp.float32)
    lam_full = jnp.exp(jnp.sum(lf[0] * lf[1])) - jnp.exp(jnp.sum(lf[2] * lf[3])) + lambda_init
    nb = S // Q_BLOCK
    qb = q.reshape(B, nb, Q_BLOCK, H, 2, dh).transpose(1, 0, 2, 3, 4, 5)

    def block(qi):
        s = jnp.einsum('bqhcd,bkhcd->bhcqk', qi, k, preferred_element_type=jnp.float32)
        p = jax.nn.softmax(s, axis=-1)
        a = p[:, :, 0] - lam_full * p[:, :, 1]
        return jnp.einsum('bhqk,bkhe->bqhe', a.astype(v.dtype), v)

    o = lax.map(block, qb)
    o = o.transpose(1, 0, 2, 3, 4).reshape(B, S, H, dv)
    o = rms_norm(o, subln_g) * (1.0 - lambda_init)
    return o.reshape(B, S, H * dv) @ w_o


def spatial_gating_mlp(h, w_uv, b_uv, ln_v_g, ln_v_b, w_s, b_s, w_o):
    B, S, _ = h.shape
    z = jax.nn.gelu(h @ w_uv + b_uv, approximate=False)
    u, v = jnp.split(z, 2, axis=-1)
    v = layer_norm(v, ln_v_g, ln_v_b)
    nc = S // SGU_CHUNK
    v = v.reshape(B, nc, SGU_CHUNK, SGU_GROUPS, SGU_GROUP_DIM)
    s = jnp.einsum('gpq,bnqgc->bnpgc', w_s, v) + b_s.T[None, None, :, :, None]
    return (u * s.reshape(B, S, SGU_WIDTH)) @ w_o


def swiglu(h, w_gu, w_down):
    g, u = jnp.split(h @ w_gu, 2, axis=-1)
    return (jax.nn.silu(g) * u) @ w_down


def moe_swiglu(h, w_router, w_e_gu, w_e_down):
    B, S, D = h.shape
    t = h.reshape(B * S, D)
    logits = (t @ w_router).astype(jnp.float32)
    top_v, top_i = lax.top_k(logits, TOP_K)
    top_w = jax.nn.softmax(top_v, axis=-1)
    combine = jnp.sum(jax.nn.one_hot(top_i, N_EXPERTS, dtype=jnp.float32) * top_w[..., None], axis=1)
    out = jnp.zeros_like(t)
    for e in range(N_EXPERTS):
        out = out + combine[:, e:e + 1].astype(t.dtype) * swiglu(t, w_e_gu[e], w_e_down[e])
    return out.reshape(B, S, D)


def trunk(x, c, w_ada, b_ada, ln_g, ln_b, w_qkv, diff_lambda, subln_g, w_o_attn,
          w_uv, b_uv, ln_v_g, ln_v_b, w_s, b_s, w_o_sgu, w_ffn_gu, w_ffn_down,
          w_router, w_e_gu, w_e_down):
    for i in range(DEPTH):
        j = i // 2
        sh1, sc1, g1, sh2, sc2, g2 = adaln_params(c, w_ada[i], b_ada[i])
        h = x * (1.0 + sc1) + sh1
        if i % 2 == 0:
            lambda_init = 0.8 - 0.6 * math.exp(-0.3 * i)
            mix = diff_attention(h, w_qkv[j], diff_lambda[j], subln_g[j], w_o_attn[j], lambda_init)
        else:
            mix = spatial_gating_mlp(h, w_uv[j], b_uv[j], ln_v_g[j], ln_v_b[j], w_s[j], b_s[j], w_o_sgu[j])
        x = layer_norm(DEEPNORM_ALPHA * x + (1.0 + g1) * mix, ln_g[i, 0], ln_b[i, 0])
        h = x * (1.0 + sc2) + sh2
        if i % 2 == 0:
            f = swiglu(h, w_ffn_gu[j], w_ffn_down[j])
        else:
            f = moe_swiglu(h, w_router[j], w_e_gu[j], w_e_down[j])
        x = layer_norm(DEEPNORM_ALPHA * x + (1.0 + g2) * f, ln_g[i, 1], ln_b[i, 1])
    return x


def setup_inputs(seed: int = 0) -> dict:
    key = jax.random.key(seed)
    ks = jax.random.split(key, 24)
    D = D_MODEL
    beta = DEEPNORM_BETA

    def nrm(k, shape, scale):
        return jax.random.normal(k, shape, jnp.float32) * scale

    return {
        "x_prompt": nrm(ks[0], (BATCH, SEQ, D), 1.0),
        "x_sample": nrm(ks[1], (DEC_BATCH, DEC_SEQ, D), 1.0),
        "c_prompt": nrm(ks[2], (BATCH, D), 1.0),
        "c_sample": nrm(ks[3], (DEC_BATCH, D), 1.0),
        "w_ada": nrm(ks[4], (DEPTH, D, 6 * D), 0.1 * D ** -0.5),
        "b_ada": nrm(ks[5], (DEPTH, 6 * D), 0.02),
        "ln_g": 1.0 + nrm(ks[6], (DEPTH, 2, D), 0.02),
        "ln_b": nrm(ks[7], (DEPTH, 2, D), 0.02),
        "w_qkv": nrm(ks[8], (N_A, D, 3 * D), D ** -0.5),
        "diff_lambda": nrm(ks[9], (N_A, 4, DIFF_HEAD_DIM), 0.1),
        "subln_g": 1.0 + nrm(ks[10], (N_A, DIFF_V_DIM), 0.02),
        "w_o_attn": nrm(ks[11], (N_A, D, D), beta * D ** -0.5),
        "w_uv": nrm(ks[12], (N_B, D, 2 * SGU_WIDTH), D ** -0.5),
        "b_uv": nrm(ks[13], (N_B, 2 * SGU_WIDTH), 0.02),
        "ln_v_g": 1.0 + nrm(ks[14], (N_B, SGU_WIDTH), 0.02),
        "ln_v_b": nrm(ks[15], (N_B, SGU_WIDTH), 0.02),
        "w_s": nrm(ks[16], (N_B, SGU_GROUPS, SGU_CHUNK, SGU_CHUNK), SGU_CHUNK ** -0.5),
        "b_s": 1.0 + nrm(ks[17], (N_B, SGU_GROUPS, SGU_CHUNK), 0.02),
        "w_o_sgu": nrm(ks[18], (N_B, SGU_WIDTH, D), beta * SGU_WIDTH ** -0.5),
        "w_ffn_gu": nrm(ks[19], (N_A, D, 2 * D_FF), D ** -0.5),
        "w_ffn_down": nrm(ks[20], (N_A, D_FF, D), beta * D_FF ** -0.5),
        "w_router": nrm(ks[21], (N_B, D, N_EXPERTS), D ** -0.5),
        "w_e_gu": nrm(ks[22], (N_B, N_EXPERTS, D, 2 * D_FF_EXPERT), D ** -0.5),
        "w_e_down": nrm(ks[23], (N_B, N_EXPERTS, D_FF_EXPERT, D), beta * D_FF_EXPERT ** -0.5),
    }


def reference(x_prompt, x_sample, c_prompt, c_sample, w_ada, b_ada, ln_g, ln_b,
              w_qkv, diff_lambda, subln_g, w_o_attn, w_uv, b_uv, ln_v_g, ln_v_b,
              w_s, b_s, w_o_sgu, w_ffn_gu, w_ffn_down, w_router, w_e_gu, w_e_down):
    y_prompt = trunk(x_prompt, c_prompt, w_ada, b_ada, ln_g, ln_b, w_qkv, diff_lambda,
                     subln_g, w_o_attn, w_uv, b_uv, ln_v_g, ln_v_b, w_s, b_s, w_o_sgu,
                     w_ffn_gu, w_ffn_down, w_router, w_e_gu, w_e_down)
    y_sample = trunk(x_sample, c_sample, w_ada, b_ada, ln_g, ln_b, w_qkv, diff_lambda,
                     subln_g, w_o_attn, w_uv, b_uv, ln_v_g, ln_v_b, w_s, b_s, w_o_sgu,
                     w_ffn_gu, w_ffn_down, w_router, w_e_gu, w_e_down)
    return (y_prompt, y_sample)
```

```python
import functools
import math

import jax
import jax.numpy as jnp
from jax import lax
from jax.experimental import pallas as pl
from jax.experimental.pallas import tpu as pltpu

F32 = jnp.float32
BF16 = jnp.bfloat16

LANES = 128
N_HEADS = 8
HEAD_DIM = 64
V_DIM = 2 * HEAD_DIM
ROPE_THETA = 10000.0
SGU_CHUNK = 128
SGU_GROUPS = 8
N_EXPERTS = 8
TOP_K = 2
LN_EPS = 1e-5
VMEM_LIMIT_BYTES = 56 * 1024 * 1024

HIGHEST = lax.Precision.HIGHEST


def _cparams(n_axes):
    return pltpu.CompilerParams(
        dimension_semantics=("arbitrary",) * n_axes,
        vmem_limit_bytes=VMEM_LIMIT_BYTES,
    )


def _ln(y, g, b):
    mu = jnp.mean(y, axis=-1, keepdims=True)
    yc = y - mu
    var = jnp.mean(yc * yc, axis=-1, keepdims=True)
    return yc * lax.rsqrt(var + LN_EPS) * g + b


def _ff_chunks(dff, chunk=512):
    out, c0 = [], 0
    while c0 < dff:
        c1 = min(c0 + chunk, dff)
        out.append((c0, c1))
        c0 = c1
    return out


def _swiglu(h, wgu_at, wd_at, dff):
    f = None
    for c0, c1 in _ff_chunks(dff):
        g = jnp.dot(h, wgu_at(c0, c1), preferred_element_type=F32)
        u = jnp.dot(h, wgu_at(dff + c0, dff + c1), preferred_element_type=F32)
        a = (g * jax.nn.sigmoid(g) * u).astype(BF16)
        p = jnp.dot(a, wd_at(c0, c1), preferred_element_type=F32)
        f = p if f is None else f + p
    return f


def _ada_kernel(c_ref, w_ref, b_ref, o_ref):
    c = c_ref[...]
    s = c * jax.nn.sigmoid(c)
    o_ref[0] = jnp.dot(s, w_ref[0], preferred_element_type=F32, precision=HIGHEST) + b_ref[0]


def _ada_call(c_all, w_ada, b_ada):
    depth, d, n6 = w_ada.shape
    nb = c_all.shape[0]
    tn = 1536
    return pl.pallas_call(
        _ada_kernel,
        grid=(depth, n6 // tn),
        in_specs=[
            pl.BlockSpec((nb, d), lambda l, j: (0, 0)),
            pl.BlockSpec((1, d, tn), lambda l, j: (l, 0, j)),
            pl.BlockSpec((1, 1, tn), lambda l, j: (l, 0, j)),
        ],
        out_specs=pl.BlockSpec((1, nb, tn), lambda l, j: (l, 0, j)),
        out_shape=jax.ShapeDtypeStruct((depth, nb, n6), F32),
        compiler_params=_cparams(2),
        name="ada",
    )(c_all, w_ada, b_ada.reshape(depth, 1, n6))


def _qkv_kernel(x_ref, mod_ref, w_ref, cq_ref, sq_ref, ck_ref, sk_ref, q_ref, k_ref, v_ref):
    d = x_ref.shape[-1]
    ts = x_ref.shape[1]
    x = x_ref[0]
    h = (x * (1.0 + mod_ref[0, 1:2, :]) + mod_ref[0, 0:1, :]).astype(BF16)
    qkv = jnp.dot(h, w_ref[...], preferred_element_type=F32)
    lane = lax.broadcasted_iota(jnp.int32, (ts, LANES), 1)
    first_half = (lane & (HEAD_DIM // 2)) == 0
    is_map1 = lane < HEAD_DIM
    cq, sq, ck, sk = cq_ref[...], sq_ref[...], ck_ref[...], sk_ref[...]

    def rope(t, cos, sin_signed):
        r = jnp.where(first_half, pltpu.roll(t, LANES - HEAD_DIM // 2, 1), pltpu.roll(t, HEAD_DIM // 2, 1))
        return t * cos + r * sin_signed

    for j in range(N_HEADS):
        qr = rope(qkv[:, j * LANES:(j + 1) * LANES], cq, sq)
        q_ref[0, 0, j] = jnp.where(is_map1, qr, 0.0).astype(BF16)
        q_ref[0, 1, j] = jnp.where(is_map1, 0.0, qr).astype(BF16)
        k_ref[0, j] = rope(qkv[:, d + j * LANES:d + (j + 1) * LANES], ck, sk).astype(BF16)
        v_ref[0, j] = qkv[:, 2 * d + j * LANES:2 * d + (j + 1) * LANES].astype(BF16)


def _rope_tables(seq):
    half = HEAD_DIM // 2
    inv = 1.0 / (ROPE_THETA ** (jnp.arange(0, HEAD_DIM, 2, dtype=F32) / HEAD_DIM))
    ang = jnp.arange(seq, dtype=F32)[:, None] * inv[None, :]
    cos = jnp.tile(jnp.cos(ang), (1, 2 * LANES // HEAD_DIM))
    sin = jnp.sin(ang)
    sin_signed = jnp.tile(jnp.concatenate([-sin, sin], axis=-1), (1, LANES // HEAD_DIM))
    del half
    scale = HEAD_DIM ** -0.5
    return cos * scale, sin_signed * scale, cos, sin_signed


def _qkv_call(x, mod, w_qkv_bf):
    b, s, d = x.shape
    ts = min(512, s)
    cq, sq, ck, sk = _rope_tables(s)
    tab = pl.BlockSpec((ts, LANES), lambda bi, si: (si, 0))
    return pl.pallas_call(
        _qkv_kernel,
        grid=(b, s // ts),
        in_specs=[
            pl.BlockSpec((1, ts, d), lambda bi, si: (bi, si, 0)),
            pl.BlockSpec((1, 6, d), lambda bi, si: (bi, 0, 0)),
            pl.BlockSpec((d, 3 * d), lambda bi, si: (0, 0)),
            tab, tab, tab, tab,
        ],
        out_specs=[
            pl.BlockSpec((1, 2, N_HEADS, ts, LANES), lambda bi, si: (bi, 0, 0, si, 0)),
            pl.BlockSpec((1, N_HEADS, ts, LANES), lambda bi, si: (bi, 0, si, 0)),
            pl.BlockSpec((1, N_HEADS, ts, LANES), lambda bi, si: (bi, 0, si, 0)),
        ],
        out_shape=[
            jax.ShapeDtypeStruct((b, 2, N_HEADS, s, LANES), BF16),
            jax.ShapeDtypeStruct((b, N_HEADS, s, LANES), BF16),
            jax.ShapeDtypeStruct((b, N_HEADS, s, LANES), BF16),
        ],
        compiler_params=_cparams(2),
        name="qkv_rope",
    )(x, mod, w_qkv_bf, cq, sq, ck, sk)


def _attn_kernel(q_ref, k_ref, v_ref, x_ref, mod_ref, lam_ref, sg_ref, wo_ref, lng_ref, lnb_ref,
                 o_ref, *, lambda_init, alpha):
    tq = x_ref.shape[1]
    lam = lam_ref[...]
    lam_full = (jnp.exp(jnp.sum(lam[0:1] * lam[1:2], keepdims=True))
                - jnp.exp(jnp.sum(lam[2:3] * lam[3:4], keepdims=True)) + lambda_init)
    heads = []
    for h in range(N_HEADS):
        qh = q_ref[0, :, h].reshape(2 * tq, LANES)
        s = lax.dot_general(qh, k_ref[0, h], (((1,), (1,)), ((), ())), preferred_element_type=F32)
        m = jnp.max(s, axis=-1, keepdims=True)
        e = jnp.exp(s - m)
        inv = 1.0 / jnp.sum(e, axis=-1, keepdims=True)
        a = (e[:tq] * inv[:tq] - e[tq:] * (inv[tq:] * lam_full)).astype(BF16)
        oh = jnp.dot(a, v_ref[0, h], preferred_element_type=F32)
        ms = jnp.mean(oh * oh, axis=-1, keepdims=True)
        oh = oh * lax.rsqrt(ms + LN_EPS) * sg_ref[...] * (1.0 - lambda_init)
        heads.append(oh.astype(BF16))
    o = jnp.concatenate(heads, axis=1)
    mix = jnp.dot(o, wo_ref[...], preferred_element_type=F32)
    y = alpha * x_ref[0] + (1.0 + mod_ref[0, 2:3, :]) * mix
    o_ref[0] = _ln(y, lng_ref[...], lnb_ref[...])


def _attn_call(q2, k, v, x, mod, lam, subln_g, wo_bf, ln_g, ln_b, lambda_init, alpha):
    b, s, d = x.shape
    tq = min(256, s)
    const2 = lambda bi, qi: (0, 0)
    return pl.pallas_call(
        functools.partial(_attn_kernel, lambda_init=lambda_init, alpha=alpha),
        grid=(b, s // tq),
        in_specs=[
            pl.BlockSpec((1, 2, N_HEADS, tq, LANES), lambda bi, qi: (bi, 0, 0, qi, 0)),
            pl.BlockSpec((1, N_HEADS, s, LANES), lambda bi, qi: (bi, 0, 0, 0)),
            pl.BlockSpec((1, N_HEADS, s, LANES), lambda bi, qi: (bi, 0, 0, 0)),
            pl.BlockSpec((1, tq, d), lambda bi, qi: (bi, qi, 0)),
            pl.BlockSpec((1, 6, d), lambda bi, qi: (bi, 0, 0)),
            pl.BlockSpec((4, HEAD_DIM), const2),
            pl.BlockSpec((1, V_DIM), const2),
            pl.BlockSpec((d, d), const2),
            pl.BlockSpec((1, d), const2),
            pl.BlockSpec((1, d), const2),
        ],
        out_specs=pl.BlockSpec((1, tq, d), lambda bi, qi: (bi, qi, 0)),
        out_shape=jax.ShapeDtypeStruct((b, s, d), F32),
        compiler_params=_cparams(2),
        name="diff_attn",
    )(q2, k, v, x, mod, lam, subln_g.reshape(1, V_DIM), wo_bf, ln_g.reshape(1, d), ln_b.reshape(1, d))


def _ffn_kernel(x_ref, mod_ref, wgu_ref, wd_ref, lng_ref, lnb_ref, o_ref, *, alpha):
    dff = wd_ref.shape[0]
    x = x_ref[0]
    h = (x * (1.0 + mod_ref[0, 4:5, :]) + mod_ref[0, 3:4, :]).astype(BF16)
    f = _swiglu(h, lambda c0, c1: wgu_ref[:, c0:c1], lambda c0, c1: wd_ref[c0:c1, :], dff)
    y = alpha * x + (1.0 + mod_ref[0, 5:6, :]) * f
    o_ref[0] = _ln(y, lng_ref[...], lnb_ref[...])


def _ffn_call(x, mod, wgu_bf, wd_bf, ln_g, ln_b, alpha):
    b, s, d = x.shape
    dff = wd_bf.shape[0]
    tm = min(512, s)
    const2 = lambda bi, si: (0, 0)
    return pl.pallas_call(
        functools.partial(_ffn_kernel, alpha=alpha),
        grid=(b, s // tm),
        in_specs=[
            pl.BlockSpec((1, tm, d), lambda bi, si: (bi, si, 0)),
            pl.BlockSpec((1, 6, d), lambda bi, si: (bi, 0, 0)),
            pl.BlockSpec((d, 2 * dff), const2, pipeline_mode=pl.Buffered(1)),
            pl.BlockSpec((dff, d), const2, pipeline_mode=pl.Buffered(1)),
            pl.BlockSpec((1, d), const2),
            pl.BlockSpec((1, d), const2),
        ],
        out_specs=pl.BlockSpec((1, tm, d), lambda bi, si: (bi, si, 0)),
        out_shape=jax.ShapeDtypeStruct((b, s, d), F32),
        compiler_params=_cparams(2),
        name="dense_swiglu",
    )(x, mod, wgu_bf, wd_bf, ln_g.reshape(1, d), ln_b.reshape(1, d))


def _sgu_kernel(x_ref, mod_ref, wuv_ref, buv_ref, lvg_ref, lvb_ref, ws_ref, bs_ref, wo_ref,
                lng_ref, lnb_ref, wr_ref, x2_ref, route_ref, *, alpha):
    tm = x_ref.shape[1]
    width = wo_ref.shape[0]
    x = x_ref[0]
    h = (x * (1.0 + mod_ref[0, 1:2, :]) + mod_ref[0, 0:1, :]).astype(BF16)
    z = jnp.dot(h, wuv_ref[...], preferred_element_type=F32) + buv_ref[...]
    z = 0.5 * z * (1.0 + lax.erf(z * (2.0 ** -0.5)))
    u = z[:, :width]
    v = _ln(z[:, width:], lvg_ref[...], lvb_ref[...]).astype(BF16)
    rows = []
    for n in range(tm // SGU_CHUNK):
        r0, r1 = n * SGU_CHUNK, (n + 1) * SGU_CHUNK
        cols = []
        for g in range(SGU_GROUPS):
            c0, c1 = g * LANES, (g + 1) * LANES
            sp = jnp.dot(ws_ref[g], v[r0:r1, c0:c1], preferred_element_type=F32) + bs_ref[g]
            cols.append(u[r0:r1, c0:c1] * sp)
        rows.append(jnp.concatenate(cols, axis=1))
    gated = jnp.concatenate(rows, axis=0).astype(BF16)
    mix = jnp.dot(gated, wo_ref[...], preferred_element_type=F32)
    y = alpha * x + (1.0 + mod_ref[0, 2:3, :]) * mix
    x2 = _ln(y, lng_ref[...], lnb_ref[...])
    x2_ref[0] = x2

    h2 = x2 * (1.0 + mod_ref[0, 4:5, :]) + mod_ref[0, 3:4, :]
    logits = jnp.dot(h2, wr_ref[...], preferred_element_type=F32, precision=HIGHEST)
    lane = lax.broadcasted_iota(jnp.int32, (tm, LANES), 1)
    neg = jnp.float32(-jnp.inf)
    logits = jnp.where(lane < N_EXPERTS, logits, neg)
    m1 = jnp.max(logits, axis=-1, keepdims=True)
    i1 = jnp.min(jnp.where(logits == m1, lane, LANES), axis=-1, keepdims=True)
    rest = jnp.where(lane == i1, neg, logits)
    m2 = jnp.max(rest, axis=-1, keepdims=True)
    i2 = jnp.min(jnp.where(rest == m2, lane, LANES), axis=-1, keepdims=True)
    t = jnp.exp(m2 - m1)
    w1 = 1.0 / (1.0 + t)
    w2 = t * w1
    route_ref[0] = jnp.where(lane == 0, i1.astype(F32),
                             jnp.where(lane == 1, i2.astype(F32),
                                       jnp.where(lane == 2, w1, jnp.where(lane == 3, w2, 0.0))))


def _sgu_call(x, mod, wuv_bf, b_uv, ln_v_g, ln_v_b, ws_bf, b_s, wo_bf, ln_g, ln_b, w_router, alpha):
    b, s, d = x.shape
    width = wo_bf.shape[0]
    tm = min(256, s)
    wr_pad = jnp.zeros((d, LANES), F32).at[:, :N_EXPERTS].set(w_router)
    const2 = lambda bi, si: (0, 0)
    const3 = lambda bi, si: (0, 0, 0)
    return pl.pallas_call(
        functools.partial(_sgu_kernel, alpha=alpha),
        grid=(b, s // tm),
        in_specs=[
            pl.BlockSpec((1, tm, d), lambda bi, si: (bi, si, 0)),
            pl.BlockSpec((1, 6, d), lambda bi, si: (bi, 0, 0)),
            pl.BlockSpec((d, 2 * width), const2),
            pl.BlockSpec((1, 2 * width), const2),
            pl.BlockSpec((1, width), const2),
            pl.BlockSpec((1, width), const2),
            pl.BlockSpec((SGU_GROUPS, SGU_CHUNK, SGU_CHUNK), const3),
            pl.BlockSpec((SGU_GROUPS, SGU_CHUNK, 1), const3),
            pl.BlockSpec((width, d), const2),
            pl.BlockSpec((1, d), const2),
            pl.BlockSpec((1, d), const2),
            pl.BlockSpec((d, LANES), const2),
        ],
        out_specs=[
            pl.BlockSpec((1, tm, d), lambda bi, si: (bi, si, 0)),
            pl.BlockSpec((1, tm, LANES), lambda bi, si: (bi, si, 0)),
        ],
        out_shape=[
            jax.ShapeDtypeStruct((b, s, d), F32),
            jax.ShapeDtypeStruct((b, s, LANES), F32),
        ],
        compiler_params=_cparams(2),
        name="sgu_router",
    )(x, mod, wuv_bf, b_uv.reshape(1, -1), ln_v_g.reshape(1, -1), ln_v_b.reshape(1, -1), ws_bf,
      b_s.reshape(SGU_GROUPS, SGU_CHUNK, 1), wo_bf, ln_g.reshape(1, d), ln_b.reshape(1, d), wr_pad)


def _dispatch_kernel(pos_ref, x_ref, mod_ref, init_ref, xs_ref, hbuf, sem):
    del init_ref
    i = pl.program_id(0)
    n = pl.num_programs(0)
    td = x_ref.shape[0]
    slot = i % 2

    def wait_slot(s):
        for _ in range(TOP_K):
            pltpu.make_async_copy(hbuf.at[s], xs_ref.at[pl.ds(0, td)], sem.at[s]).wait()

    @pl.when(i >= 2)
    def _():
        wait_slot(slot)

    hbuf[slot] = x_ref[...] * (1.0 + mod_ref[0, 4:5, :]) + mod_ref[0, 3:4, :]

    def body(r, carry):
        for kk in range(TOP_K):
            p = pos_ref[0, 0, TOP_K * r + kk]
            pltpu.make_async_copy(hbuf.at[slot, pl.ds(r, 1)], xs_ref.at[pl.ds(p, 1)], sem.at[slot]).start()
        return carry

    lax.fori_loop(0, td, body, 0, unroll=8)

    @pl.when(i == n - 1)
    def _():
        wait_slot(slot)

    @pl.when(jnp.logical_and(i == n - 1, n >= 2))
    def _():
        wait_slot(1 - slot)


def _dispatch_call(x2_flat, mod, pos3, n_rows, tiles_per_batch):
    t, d = x2_flat.shape
    td = pos3.shape[-1] // TOP_K
    init = jnp.zeros((n_rows, d), F32)
    return pl.pallas_call(
        _dispatch_kernel,
        grid=(t // td,),
        in_specs=[
            pl.BlockSpec((1, 1, TOP_K * td), lambda i: (i, 0, 0), memory_space=pltpu.SMEM),
            pl.BlockSpec((td, d), lambda i: (i, 0)),
            pl.BlockSpec((1, 6, d), lambda i: (i // tiles_per_batch, 0, 0)),
            pl.BlockSpec(memory_space=pl.ANY),
        ],
        out_specs=pl.BlockSpec(memory_space=pl.ANY),
        out_shape=jax.ShapeDtypeStruct((n_rows, d), F32),
        scratch_shapes=[pltpu.VMEM((2, td, d), F32), pltpu.SemaphoreType.DMA((2,))],
        input_output_aliases={3: 0},
        compiler_params=_cparams(1),
        name="moe_dispatch",
    )(pos3, x2_flat, mod, init)


def _expert_kernel(te_ref, tv_ref, xs_ref, wgu_ref, wd_ref, ys_ref):
    del te_ref
    t = pl.program_id(0)
    dff = wd_ref.shape[1]

    @pl.when(tv_ref[t] != 0)
    def _():
        h = xs_ref[...].astype(BF16)
        ys_ref[...] = _swiglu(h, lambda c0, c1: wgu_ref[0, :, c0:c1], lambda c0, c1: wd_ref[0, c0:c1, :], dff)

    @pl.when(tv_ref[t] == 0)
    def _():
        ys_ref[...] = jnp.zeros_like(ys_ref)


def _expert_call(tile_expert, tile_valid, xs, wegu_bf, wed_bf, tm):
    n_rows, d = xs.shape
    dff = wed_bf.shape[1]
    grid_spec = pltpu.PrefetchScalarGridSpec(
        num_scalar_prefetch=2,
        grid=(n_rows // tm,),
        in_specs=[
            pl.BlockSpec((tm, d), lambda t, te, tv: (t, 0)),
            pl.BlockSpec((1, d, 2 * dff), lambda t, te, tv: (te[t], 0, 0)),
            pl.BlockSpec((1, dff, d), lambda t, te, tv: (te[t], 0, 0)),
        ],
        out_specs=pl.BlockSpec((tm, d), lambda t, te, tv: (t, 0)),
    )
    return pl.pallas_call(
        _expert_kernel,
        grid_spec=grid_spec,
        out_shape=jax.ShapeDtypeStruct((n_rows, d), F32),
        compiler_params=_cparams(1),
        name="moe_experts",
    )(tile_expert, tile_valid, xs, wegu_bf, wed_bf)


def _combine_kernel(pos_ref, posn_ref, x_ref, mod_ref, route_ref, lng_ref, lnb_ref, ys_ref, o_ref,
                    ybuf, sem, *, alpha):
    i = pl.program_id(0)
    n = pl.num_programs(0)
    tc = x_ref.shape[0]
    slot = i % 2

    def issue(pref, s):
        def body(r, carry):
            for kk in range(TOP_K):
                p = pref[0, 0, TOP_K * r + kk]
                pltpu.make_async_copy(ys_ref.at[pl.ds(p, 1)], ybuf.at[s, kk, pl.ds(r, 1)], sem.at[s]).start()
            return carry
        lax.fori_loop(0, tc, body, 0, unroll=8)

    @pl.when(i == 0)
    def _():
        issue(pos_ref, 0)

    @pl.when(i + 1 < n)
    def _():
        issue(posn_ref, 1 - slot)

    for kk in range(TOP_K):
        pltpu.make_async_copy(ys_ref.at[pl.ds(0, tc)], ybuf.at[slot, kk], sem.at[slot]).wait()

    x = x_ref[...]
    f = route_ref[:, 2:3] * ybuf[slot, 0] + route_ref[:, 3:4] * ybuf[slot, 1]
    y = alpha * x + (1.0 + mod_ref[0, 5:6, :]) * f
    o_ref[...] = _ln(y, lng_ref[...], lnb_ref[...])


def _combine_call(x2_flat, mod, route_flat, pos3, ys, ln_g, ln_b, tiles_per_batch, alpha):
    t, d = x2_flat.shape
    tc = pos3.shape[-1] // TOP_K
    n = t // tc
    return pl.pallas_call(
        functools.partial(_combine_kernel, alpha=alpha),
        grid=(n,),
        in_specs=[
            pl.BlockSpec((1, 1, TOP_K * tc), lambda i: (i, 0, 0), memory_space=pltpu.SMEM),
            pl.BlockSpec((1, 1, TOP_K * tc), lambda i: (jnp.minimum(i + 1, n - 1), 0, 0),
                         memory_space=pltpu.SMEM),
            pl.BlockSpec((tc, d), lambda i: (i, 0)),
            pl.BlockSpec((1, 6, d), lambda i: (i // tiles_per_batch, 0, 0)),
            pl.BlockSpec((tc, LANES), lambda i: (i, 0)),
            pl.BlockSpec((1, d), lambda i: (0, 0)),
            pl.BlockSpec((1, d), lambda i: (0, 0)),
            pl.BlockSpec(memory_space=pl.ANY),
        ],
        out_specs=pl.BlockSpec((tc, d), lambda i: (i, 0)),
        out_shape=jax.ShapeDtypeStruct((t, d), F32),
        scratch_shapes=[pltpu.VMEM((2, TOP_K, tc, d), F32), pltpu.SemaphoreType.DMA((2,))],
        compiler_params=_cparams(1),
        name="moe_combine",
    )(pos3, pos3, x2_flat, mod, route_flat, ln_g.reshape(1, d), ln_b.reshape(1, d), ys)


def _moe(x2, mod, route, wegu_bf, wed_bf, ln_g, ln_b, alpha):
    b, s, d = x2.shape
    t = b * s
    tm = 256
    tr = min(256, s)
    n_tiles = (t * TOP_K) // tm + N_EXPERTS
    r = route.reshape(t, LANES)
    e_flat = r[:, :TOP_K].astype(jnp.int32).reshape(t * TOP_K)
    onehot = (e_flat[:, None] == jnp.arange(N_EXPERTS, dtype=jnp.int32)[None, :]).astype(jnp.int32)
    csum = jnp.cumsum(onehot, axis=0)
    rank = jnp.sum(onehot * (csum - 1), axis=1)
    counts = csum[-1]
    padded = ((counts + tm - 1) // tm) * tm
    ends = jnp.cumsum(padded)
    starts = ends - padded
    pos = (starts[e_flat] + rank).astype(jnp.int32)
    tile_start = jnp.arange(n_tiles, dtype=jnp.int32) * tm
    tile_expert = jnp.minimum(jnp.sum((tile_start[:, None] >= ends[None, :]).astype(jnp.int32), axis=1),
                              N_EXPERTS - 1).astype(jnp.int32)
    tile_valid = (tile_start < ends[-1]).astype(jnp.int32)
    pos3 = pos.reshape(t // tr, 1, TOP_K * tr)
    x2_flat = x2.reshape(t, d)
    xs = _dispatch_call(x2_flat, mod, pos3, n_tiles * tm, s // tr)
    ys = _expert_call(tile_expert, tile_valid, xs, wegu_bf, wed_bf, tm)
    out = _combine_call(x2_flat, mod, r, pos3, ys, ln_g, ln_b, s // tr, alpha)
    return out.reshape(b, s, d)


def kernel(x_prompt, x_sample, c_prompt, c_sample, w_ada, b_ada, ln_g, ln_b, w_qkv, diff_lambda,
           subln_g, w_o_attn, w_uv, b_uv, ln_v_g, ln_v_b, w_s, b_s, w_o_sgu, w_ffn_gu, w_ffn_down,
           w_router, w_e_gu, w_e_down):
    depth = w_ada.shape[0]
    d = x_prompt.shape[-1]
    alpha = (2 * depth) ** 0.25
    bp = c_prompt.shape[0]
    mods = _ada_call(jnp.concatenate([c_prompt, c_sample], axis=0), w_ada, b_ada)

    w_qkv_bf = w_qkv.astype(BF16)
    w_o_attn_bf = w_o_attn.astype(BF16)
    w_ffn_gu_bf = w_ffn_gu.astype(BF16)
    w_ffn_down_bf = w_ffn_down.astype(BF16)
    w_uv_bf = w_uv.astype(BF16)
    w_s_bf = w_s.astype(BF16)
    w_o_sgu_bf = w_o_sgu.astype(BF16)
    w_e_gu_bf = w_e_gu.astype(BF16)
    w_e_down_bf = w_e_down.astype(BF16)

    def trunk(x, r0, r1):
        for i in range(depth):
            j = i // 2
            mod = mods[i, r0:r1].reshape(r1 - r0, 6, d)
            if i % 2 == 0:
                lambda_init = 0.8 - 0.6 * math.exp(-0.3 * i)
                q2, k, v = _qkv_call(x, mod, w_qkv_bf[j])
                x = _attn_call(q2, k, v, x, mod, diff_lambda[j], subln_g[j], w_o_attn_bf[j],
                               ln_g[i, 0], ln_b[i, 0], lambda_init, alpha)
                x = _ffn_call(x, mod, w_ffn_gu_bf[j], w_ffn_down_bf[j], ln_g[i, 1], ln_b[i, 1], alpha)
            else:
                x, route = _sgu_call(x, mod, w_uv_bf[j], b_uv[j], ln_v_g[j], ln_v_b[j], w_s_bf[j], b_s[j],
                                     w_o_sgu_bf[j], ln_g[i, 0], ln_b[i, 0], w_router[j], alpha)
                x = _moe(x, mod, route, w_e_gu_bf[j], w_e_down_bf[j], ln_g[i, 1], ln_b[i, 1], alpha)
        return x

    y_prompt = trunk(x_prompt, 0, bp)
    y_sample = trunk(x_sample, bp, bp + c_sample.shape[0])
    return (y_prompt, y_sample)
```

```python
import functools
import math

import jax
import jax.numpy as jnp
from jax import lax
from jax.experimental import pallas as pl
from jax.experimental.pallas import tpu as pltpu

F32 = jnp.float32
BF16 = jnp.bfloat16

LANES = 128
N_HEADS = 8
HEAD_DIM = 64
V_DIM = 2 * HEAD_DIM
ROPE_THETA = 10000.0
SGU_CHUNK = 128
SGU_GROUPS = 8
N_EXPERTS = 8
TOP_K = 2
LN_EPS = 1e-5
VMEM_LIMIT_BYTES = 56 * 1024 * 1024

HIGHEST = lax.Precision.HIGHEST
LOG2_E = 1.4426950408889634


def _cparams(n_axes):
    return pltpu.CompilerParams(
        dimension_semantics=("arbitrary",) * n_axes,
        vmem_limit_bytes=VMEM_LIMIT_BYTES,
    )


def _ln(y, g, b):
    mu = jnp.mean(y, axis=-1, keepdims=True)
    yc = y - mu
    var = jnp.mean(yc * yc, axis=-1, keepdims=True)
    return yc * lax.rsqrt(var + LN_EPS) * g + b


def _ff_chunks(dff, chunk=512):
    out, c0 = [], 0
    while c0 < dff:
        c1 = min(c0 + chunk, dff)
        out.append((c0, c1))
        c0 = c1
    return out


def _swiglu(h, wgu_at, wd_at, dff):
    f = None
    for c0, c1 in _ff_chunks(dff):
        g = jnp.dot(h, wgu_at(c0, c1), preferred_element_type=F32)
        u = jnp.dot(h, wgu_at(dff + c0, dff + c1), preferred_element_type=F32)
        a = (g * jax.nn.sigmoid(g) * u).astype(BF16)
        p = jnp.dot(a, wd_at(c0, c1), preferred_element_type=F32)
        f = p if f is None else f + p
    return f


def _ada_kernel(c_ref, w_ref, b_ref, o_ref):
    c = c_ref[...]
    s = c * jax.nn.sigmoid(c)
    o_ref[0] = jnp.dot(s, w_ref[0], preferred_element_type=F32, precision=HIGHEST) + b_ref[0]


def _ada_call(c_all, w_ada, b_ada):
    depth, d, n6 = w_ada.shape
    nb = c_all.shape[0]
    tn = 1536
    return pl.pallas_call(
        _ada_kernel,
        grid=(depth, n6 // tn),
        in_specs=[
            pl.BlockSpec((nb, d), lambda l, j: (0, 0)),
            pl.BlockSpec((1, d, tn), lambda l, j: (l, 0, j)),
            pl.BlockSpec((1, 1, tn), lambda l, j: (l, 0, j)),
        ],
        out_specs=pl.BlockSpec((1, nb, tn), lambda l, j: (l, 0, j)),
        out_shape=jax.ShapeDtypeStruct((depth, nb, n6), F32),
        compiler_params=_cparams(2),
        name="ada",
    )(c_all, w_ada, b_ada.reshape(depth, 1, n6))


def _qkv_kernel(x_ref, mod_ref, w_ref, cq_ref, sq_ref, ck_ref, sk_ref, q_ref, k_ref, v_ref):
    d = x_ref.shape[-1]
    ts = x_ref.shape[1]
    x = x_ref[0]
    h = (x * (1.0 + mod_ref[0, 1:2, :]) + mod_ref[0, 0:1, :]).astype(BF16)
    qkv = jnp.dot(h, w_ref[...], preferred_element_type=F32)
    lane = lax.broadcasted_iota(jnp.int32, (ts, LANES), 1)
    first_half = (lane & (HEAD_DIM // 2)) == 0
    is_map1 = lane < HEAD_DIM
    cq, sq, ck, sk = cq_ref[...], sq_ref[...], ck_ref[...], sk_ref[...]

    def rope(t, cos, sin_signed):
        r = jnp.where(first_half, pltpu.roll(t, LANES - HEAD_DIM // 2, 1), pltpu.roll(t, HEAD_DIM // 2, 1))
        return t * cos + r * sin_signed

    for j in range(N_HEADS):
        qr = rope(qkv[:, j * LANES:(j + 1) * LANES], cq, sq)
        q_ref[0, 0, j] = jnp.where(is_map1, qr, 0.0).astype(BF16)
        q_ref[0, 1, j] = jnp.where(is_map1, 0.0, qr).astype(BF16)
        k_ref[0, j] = rope(qkv[:, d + j * LANES:d + (j + 1) * LANES], ck, sk).astype(BF16)
        v_ref[0, j] = qkv[:, 2 * d + j * LANES:2 * d + (j + 1) * LANES].astype(BF16)


def _rope_tables(seq):
    half = HEAD_DIM // 2
    inv = 1.0 / (ROPE_THETA ** (jnp.arange(0, HEAD_DIM, 2, dtype=F32) / HEAD_DIM))
    ang = jnp.arange(seq, dtype=F32)[:, None] * inv[None, :]
    cos = jnp.tile(jnp.cos(ang), (1, 2 * LANES // HEAD_DIM))
    sin = jnp.sin(ang)
    sin_signed = jnp.tile(jnp.concatenate([-sin, sin], axis=-1), (1, LANES // HEAD_DIM))
    del half
    scale = HEAD_DIM ** -0.5 * LOG2_E
    return cos * scale, sin_signed * scale, cos, sin_signed


def _qkv_call(x, mod, w_qkv_bf):
    b, s, d = x.shape
    ts = min(512, s)
    cq, sq, ck, sk = _rope_tables(s)
    tab = pl.BlockSpec((ts, LANES), lambda bi, si: (si, 0))
    return pl.pallas_call(
        _qkv_kernel,
        grid=(b, s // ts),
        in_specs=[
            pl.BlockSpec((1, ts, d), lambda bi, si: (bi, si, 0)),
            pl.BlockSpec((1, 6, d), lambda bi, si: (bi, 0, 0)),
            pl.BlockSpec((d, 3 * d), lambda bi, si: (0, 0)),
            tab, tab, tab, tab,
        ],
        out_specs=[
            pl.BlockSpec((1, 2, N_HEADS, ts, LANES), lambda bi, si: (bi, 0, 0, si, 0)),
            pl.BlockSpec((1, N_HEADS, ts, LANES), lambda bi, si: (bi, 0, si, 0)),
            pl.BlockSpec((1, N_HEADS, ts, LANES), lambda bi, si: (bi, 0, si, 0)),
        ],
        out_shape=[
            jax.ShapeDtypeStruct((b, 2, N_HEADS, s, LANES), BF16),
            jax.ShapeDtypeStruct((b, N_HEADS, s, LANES), BF16),
            jax.ShapeDtypeStruct((b, N_HEADS, s, LANES), BF16),
        ],
        compiler_params=_cparams(2),
        name="qkv_rope",
    )(x, mod, w_qkv_bf, cq, sq, ck, sk)


def _attn_kernel(q_ref, k_ref, v_ref, x_ref, mod_ref, lam_ref, sg_ref, wo_ref, lng_ref, lnb_ref,
                 o_ref, *, lambda_init, alpha):
    tq = x_ref.shape[1]
    lam = lam_ref[...]
    lam_full = (jnp.exp(jnp.sum(lam[0:1] * lam[1:2], keepdims=True))
                - jnp.exp(jnp.sum(lam[2:3] * lam[3:4], keepdims=True)) + lambda_init)
    def scores(h):
        qh = q_ref[0, :, h].reshape(2 * tq, LANES)
        return lax.dot_general(qh, k_ref[0, h], (((1,), (1,)), ((), ())), preferred_element_type=F32)

    heads = []
    s_next = scores(0)
    for h in range(N_HEADS):
        s = s_next
        if h + 1 < N_HEADS:
            s_next = scores(h + 1)
        m = jnp.max(s, axis=-1, keepdims=True)
        e = jnp.exp2(s - m)
        l = jnp.sum(e, axis=-1, keepdims=True)
        a = (e[:tq] - e[tq:] * (lam_full * l[:tq] / l[tq:])).astype(BF16)
        oh = jnp.dot(a, v_ref[0, h], preferred_element_type=F32) * (1.0 / l[:tq])
        ms = jnp.mean(oh * oh, axis=-1, keepdims=True)
        oh = oh * lax.rsqrt(ms + LN_EPS) * sg_ref[...] * (1.0 - lambda_init)
        heads.append(oh.astype(BF16))
    o = jnp.concatenate(heads, axis=1)
    mix = jnp.dot(o, wo_ref[...], preferred_element_type=F32)
    y = alpha * x_ref[0] + (1.0 + mod_ref[0, 2:3, :]) * mix
    o_ref[0] = _ln(y, lng_ref[...], lnb_ref[...])


def _attn_call(q2, k, v, x, mod, lam, subln_g, wo_bf, ln_g, ln_b, lambda_init, alpha):
    b, s, d = x.shape
    tq = min(256, s)
    const2 = lambda bi, qi: (0, 0)
    return pl.pallas_call(
        functools.partial(_attn_kernel, lambda_init=lambda_init, alpha=alpha),
        grid=(b, s // tq),
        in_specs=[
            pl.BlockSpec((1, 2, N_HEADS, tq, LANES), lambda bi, qi: (bi, 0, 0, qi, 0)),
            pl.BlockSpec((1, N_HEADS, s, LANES), lambda bi, qi: (bi, 0, 0, 0)),
            pl.BlockSpec((1, N_HEADS, s, LANES), lambda bi, qi: (bi, 0, 0, 0)),
            pl.BlockSpec((1, tq, d), lambda bi, qi: (bi, qi, 0)),
            pl.BlockSpec((1, 6, d), lambda bi, qi: (bi, 0, 0)),
            pl.BlockSpec((4, HEAD_DIM), const2),
            pl.BlockSpec((1, V_DIM), const2),
            pl.BlockSpec((d, d), const2),
            pl.BlockSpec((1, d), const2),
            pl.BlockSpec((1, d), const2),
        ],
        out_specs=pl.BlockSpec((1, tq, d), lambda bi, qi: (bi, qi, 0)),
        out_shape=jax.ShapeDtypeStruct((b, s, d), F32),
        compiler_params=_cparams(2),
        name="diff_attn",
    )(q2, k, v, x, mod, lam, subln_g.reshape(1, V_DIM), wo_bf, ln_g.reshape(1, d), ln_b.reshape(1, d))


def _ffn_kernel(x_ref, mod_ref, wgu_ref, wd_ref, lng_ref, lnb_ref, o_ref, *, alpha):
    dff = wd_ref.shape[0]
    x = x_ref[0]
    h = (x * (1.0 + mod_ref[0, 4:5, :]) + mod_ref[0, 3:4, :]).astype(BF16)
    f = _swiglu(h, lambda c0, c1: wgu_ref[:, c0:c1], lambda c0, c1: wd_ref[c0:c1, :], dff)
    y = alpha * x + (1.0 + mod_ref[0, 5:6, :]) * f
    o_ref[0] = _ln(y, lng_ref[...], lnb_ref[...])


def _ffn_call(x, mod, wgu_bf, wd_bf, ln_g, ln_b, alpha):
    b, s, d = x.shape
    dff = wd_bf.shape[0]
    tm = min(512, s)
    const2 = lambda bi, si: (0, 0)
    return pl.pallas_call(
        functools.partial(_ffn_kernel, alpha=alpha),
        grid=(b, s // tm),
        in_specs=[
            pl.BlockSpec((1, tm, d), lambda bi, si: (bi, si, 0)),
            pl.BlockSpec((1, 6, d), lambda bi, si: (bi, 0, 0)),
            pl.BlockSpec((d, 2 * dff), const2, pipeline_mode=pl.Buffered(1)),
            pl.BlockSpec((dff, d), const2, pipeline_mode=pl.Buffered(1)),
            pl.BlockSpec((1, d), const2),
            pl.BlockSpec((1, d), const2),
        ],
        out_specs=pl.BlockSpec((1, tm, d), lambda bi, si: (bi, si, 0)),
        out_shape=jax.ShapeDtypeStruct((b, s, d), F32),
        compiler_params=_cparams(2),
        name="dense_swiglu",
    )(x, mod, wgu_bf, wd_bf, ln_g.reshape(1, d), ln_b.reshape(1, d))


def _sgu_kernel(x_ref, mod_ref, wuv_ref, buv_ref, lvg_ref, lvb_ref, ws_ref, bs_ref, wo_ref,
                lng_ref, lnb_ref, wr_ref, x2_ref, route_ref, *, alpha, n_sub):
    tm = x_ref.shape[1]
    sub = tm // n_sub
    width = wo_ref.shape[0]
    blocks = [(i * sub, (i + 1) * sub) for i in range(n_sub)]
    xs = [x_ref[0, r0:r1, :] for r0, r1 in blocks]
    hs = [(x * (1.0 + mod_ref[0, 1:2, :]) + mod_ref[0, 0:1, :]).astype(BF16) for x in xs]
    zs = [jnp.dot(h, wuv_ref[...], preferred_element_type=F32) + buv_ref[...] for h in hs]
    zs = [0.5 * z * (1.0 + lax.erf(z * (2.0 ** -0.5))) for z in zs]
    us = [z[:, :width] for z in zs]
    vs = [_ln(z[:, width:], lvg_ref[...], lvb_ref[...]).astype(BF16) for z in zs]

    def spatial(u, v):
        rows = []
        for n in range(sub // SGU_CHUNK):
            r0, r1 = n * SGU_CHUNK, (n + 1) * SGU_CHUNK
            cols = []
            for g in range(SGU_GROUPS):
                c0, c1 = g * LANES, (g + 1) * LANES
                sp = jnp.dot(ws_ref[g], v[r0:r1, c0:c1], preferred_element_type=F32) + bs_ref[g]
                cols.append(u[r0:r1, c0:c1] * sp)
            rows.append(jnp.concatenate(cols, axis=1))
        return jnp.concatenate(rows, axis=0).astype(BF16)

    gated = [spatial(u, v) for u, v in zip(us, vs)]
    mixes = [jnp.dot(g, wo_ref[...], preferred_element_type=F32) for g in gated]
    x2s = [_ln(alpha * x + (1.0 + mod_ref[0, 2:3, :]) * mix, lng_ref[...], lnb_ref[...])
           for x, mix in zip(xs, mixes)]
    for (r0, r1), x2 in zip(blocks, x2s):
        x2_ref[0, r0:r1, :] = x2

    def split_hi_lo(x2):
        h2 = x2 * (1.0 + mod_ref[0, 4:5, :]) + mod_ref[0, 3:4, :]
        h_hi = h2.astype(BF16)
        h_lo = (h2 - h_hi.astype(F32)).astype(BF16)
        return jnp.concatenate([h_hi, h_lo], axis=0)

    prods = [jnp.dot(split_hi_lo(x2), wr_ref[...], preferred_element_type=F32) for x2 in x2s]
    lane = lax.broadcasted_iota(jnp.int32, (sub, LANES), 1)
    neg = jnp.float32(-jnp.inf)
    for (r0, r1), p in zip(blocks, prods):
        both = p[:sub] + p[sub:]
        logits = both + pltpu.roll(both, LANES - N_EXPERTS, 1)
        logits = jnp.where(lane < N_EXPERTS, logits, neg)
        m1 = jnp.max(logits, axis=-1, keepdims=True)
        i1 = jnp.min(jnp.where(logits == m1, lane, LANES), axis=-1, keepdims=True)
        rest = jnp.where(lane == i1, neg, logits)
        m2 = jnp.max(rest, axis=-1, keepdims=True)
        i2 = jnp.min(jnp.where(rest == m2, lane, LANES), axis=-1, keepdims=True)
        t = jnp.exp(m2 - m1)
        w1 = 1.0 / (1.0 + t)
        w2 = t * w1
        route_ref[0, r0:r1, :] = jnp.where(
            lane == 0, i1.astype(F32),
            jnp.where(lane == 1, i2.astype(F32), jnp.where(lane == 2, w1, jnp.where(lane == 3, w2, 0.0))))


def _sgu_call(x, mod, wuv_bf, b_uv, ln_v_g, ln_v_b, ws_bf, b_s, wo_bf, ln_g, ln_b, w_router, alpha):
    b, s, d = x.shape
    width = wo_bf.shape[0]
    tm = min(512, s)
    n_sub = 2 if tm % (2 * SGU_CHUNK) == 0 else 1
    w_hi = w_router.astype(BF16)
    w_lo = (w_router - w_hi.astype(F32)).astype(BF16)
    wr_pad = (jnp.zeros((d, LANES), BF16).at[:, :N_EXPERTS].set(w_hi)
              .at[:, N_EXPERTS:2 * N_EXPERTS].set(w_lo))
    const2 = lambda bi, si: (0, 0)
    const3 = lambda bi, si: (0, 0, 0)
    return pl.pallas_call(
        functools.partial(_sgu_kernel, alpha=alpha, n_sub=n_sub),
        grid=(b, s // tm),
        in_specs=[
            pl.BlockSpec((1, tm, d), lambda bi, si: (bi, si, 0)),
            pl.BlockSpec((1, 6, d), lambda bi, si: (bi, 0, 0)),
            pl.BlockSpec((d, 2 * width), const2),
            pl.BlockSpec((1, 2 * width), const2),
            pl.BlockSpec((1, width), const2),
            pl.BlockSpec((1, width), const2),
            pl.BlockSpec((SGU_GROUPS, SGU_CHUNK, SGU_CHUNK), const3),
            pl.BlockSpec((SGU_GROUPS, SGU_CHUNK, 1), const3),
            pl.BlockSpec((width, d), const2),
            pl.BlockSpec((1, d), const2),
            pl.BlockSpec((1, d), const2),
            pl.BlockSpec((d, LANES), const2),
        ],
        out_specs=[
            pl.BlockSpec((1, tm, d), lambda bi, si: (bi, si, 0)),
            pl.BlockSpec((1, tm, LANES), lambda bi, si: (bi, si, 0)),
        ],
        out_shape=[
            jax.ShapeDtypeStruct((b, s, d), F32),
            jax.ShapeDtypeStruct((b, s, LANES), F32),
        ],
        compiler_params=_cparams(2),
        name="sgu_router",
    )(x, mod, wuv_bf, b_uv.reshape(1, -1), ln_v_g.reshape(1, -1), ln_v_b.reshape(1, -1), ws_bf,
      b_s.reshape(SGU_GROUPS, SGU_CHUNK, 1), wo_bf, ln_g.reshape(1, d), ln_b.reshape(1, d), wr_pad)


def _dispatch_kernel(pos_ref, zfill_ref, x_ref, mod_ref, xs_ref, hbuf, sem):
    i = pl.program_id(0)
    n = pl.num_programs(0)
    td = x_ref.shape[0]
    slot = i % 2

    def wait_slot(s):
        for _ in range(TOP_K):
            pltpu.make_async_copy(hbuf.at[s], xs_ref.at[pl.ds(0, td)], sem.at[s]).wait()

    @pl.when(i == 0)
    def _():
        hbuf[1] = jnp.zeros((td, x_ref.shape[1]), F32)
        def zero_tile(e):
            start = pl.multiple_of(zfill_ref[0, e], td)
            return pltpu.make_async_copy(hbuf.at[1], xs_ref.at[pl.ds(start, td)], sem.at[2])

        for e in range(zfill_ref.shape[1]):
            @pl.when(zfill_ref[1, e] != 0)
            def _():
                zero_tile(e).start()
        for e in range(zfill_ref.shape[1]):
            @pl.when(zfill_ref[1, e] != 0)
            def _():
                zero_tile(e).wait()

    @pl.when(i >= 2)
    def _():
        wait_slot(slot)

    hbuf[slot] = x_ref[...] * (1.0 + mod_ref[0, 4:5, :]) + mod_ref[0, 3:4, :]

    def body(r, carry):
        for kk in range(TOP_K):
            p = pos_ref[0, 0, TOP_K * r + kk]
            pltpu.make_async_copy(hbuf.at[slot, pl.ds(r, 1)], xs_ref.at[pl.ds(p, 1)], sem.at[slot]).start()
        return carry

    lax.fori_loop(0, td, body, 0, unroll=8)

    @pl.when(i == n - 1)
    def _():
        wait_slot(slot)

    @pl.when(jnp.logical_and(i == n - 1, n >= 2))
    def _():
        wait_slot(1 - slot)


def _dispatch_call(x2_flat, mod, pos3, zfill, n_rows, tiles_per_batch):
    t, d = x2_flat.shape
    td = pos3.shape[-1] // TOP_K
    return pl.pallas_call(
        _dispatch_kernel,
        grid=(t // td,),
        in_specs=[
            pl.BlockSpec((1, 1, TOP_K * td), lambda i: (i, 0, 0), memory_space=pltpu.SMEM),
            pl.BlockSpec(memory_space=pltpu.SMEM),
            pl.BlockSpec((td, d), lambda i: (i, 0)),
            pl.BlockSpec((1, 6, d), lambda i: (i // tiles_per_batch, 0, 0)),
        ],
        out_specs=pl.BlockSpec(memory_space=pl.ANY),
        out_shape=jax.ShapeDtypeStruct((n_rows, d), F32),
        scratch_shapes=[pltpu.VMEM((2, td, d), F32), pltpu.SemaphoreType.DMA((3,))],
        compiler_params=_cparams(1),
        name="moe_dispatch",
    )(pos3, zfill, x2_flat, mod)


def _expert_kernel(te_ref, nv_ref, xs_ref, wgu_ref, wd_ref, ys_ref):
    del te_ref
    t = pl.program_id(0)
    dff = wd_ref.shape[1]

    @pl.when(t < nv_ref[0])
    def _():
        h = xs_ref[...].astype(BF16)
        ys_ref[...] = _swiglu(h, lambda c0, c1: wgu_ref[0, :, c0:c1], lambda c0, c1: wd_ref[0, c0:c1, :], dff)

    @pl.when(t >= nv_ref[0])
    def _():
        ys_ref[...] = jnp.zeros_like(ys_ref)


def _expert_call(tile_expert, n_valid, xs, wegu_bf, wed_bf, tm):
    n_rows, d = xs.shape
    dff = wed_bf.shape[1]
    grid_spec = pltpu.PrefetchScalarGridSpec(
        num_scalar_prefetch=2,
        grid=(n_rows // tm,),
        in_specs=[
            pl.BlockSpec((tm, d), lambda t, te, nv: (jnp.minimum(t, nv[0] - 1), 0)),
            pl.BlockSpec((1, d, 2 * dff), lambda t, te, nv: (te[t], 0, 0)),
            pl.BlockSpec((1, dff, d), lambda t, te, nv: (te[t], 0, 0)),
        ],
        out_specs=pl.BlockSpec((tm, d), lambda t, te, nv: (t, 0)),
    )
    return pl.pallas_call(
        _expert_kernel,
        grid_spec=grid_spec,
        out_shape=jax.ShapeDtypeStruct((n_rows, d), F32),
        compiler_params=_cparams(1),
        name="moe_experts",
    )(tile_expert, n_valid, xs, wegu_bf, wed_bf)


def _combine_kernel(pos_ref, posn_ref, x_ref, mod_ref, route_ref, lng_ref, lnb_ref, ys_ref, o_ref,
                    ybuf, sem, *, alpha):
    i = pl.program_id(0)
    n = pl.num_programs(0)
    tc = x_ref.shape[0]
    slot = i % 2

    def issue(pref, s):
        def body(r, carry):
            for kk in range(TOP_K):
                p = pref[0, 0, TOP_K * r + kk]
                pltpu.make_async_copy(ys_ref.at[pl.ds(p, 1)], ybuf.at[s, kk, pl.ds(r, 1)], sem.at[s]).start()
            return carry
        lax.fori_loop(0, tc, body, 0, unroll=8)

    @pl.when(i == 0)
    def _():
        issue(pos_ref, 0)

    @pl.when(i + 1 < n)
    def _():
        issue(posn_ref, 1 - slot)

    for kk in range(TOP_K):
        pltpu.make_async_copy(ys_ref.at[pl.ds(0, tc)], ybuf.at[slot, kk], sem.at[slot]).wait()

    x = x_ref[...]
    f = route_ref[:, 2:3] * ybuf[slot, 0] + route_ref[:, 3:4] * ybuf[slot, 1]
    y = alpha * x + (1.0 + mod_ref[0, 5:6, :]) * f
    o_ref[...] = _ln(y, lng_ref[...], lnb_ref[...])


def _combine_call(x2_flat, mod, route_flat, pos3, ys, ln_g, ln_b, tiles_per_batch, alpha):
    t, d = x2_flat.shape
    tc = pos3.shape[-1] // TOP_K
    n = t // tc
    return pl.pallas_call(
        functools.partial(_combine_kernel, alpha=alpha),
        grid=(n,),
        in_specs=[
            pl.BlockSpec((1, 1, TOP_K * tc), lambda i: (i, 0, 0), memory_space=pltpu.SMEM),
            pl.BlockSpec((1, 1, TOP_K * tc), lambda i: (jnp.minimum(i + 1, n - 1), 0, 0),
                         memory_space=pltpu.SMEM),
            pl.BlockSpec((tc, d), lambda i: (i, 0)),
            pl.BlockSpec((1, 6, d), lambda i: (i // tiles_per_batch, 0, 0)),
            pl.BlockSpec((tc, LANES), lambda i: (i, 0)),
            pl.BlockSpec((1, d), lambda i: (0, 0)),
            pl.BlockSpec((1, d), lambda i: (0, 0)),
            pl.BlockSpec(memory_space=pl.ANY),
        ],
        out_specs=pl.BlockSpec((tc, d), lambda i: (i, 0)),
        out_shape=jax.ShapeDtypeStruct((t, d), F32),
        scratch_shapes=[pltpu.VMEM((2, TOP_K, tc, d), F32), pltpu.SemaphoreType.DMA((2,))],
        compiler_params=_cparams(1),
        name="moe_combine",
    )(pos3, pos3, x2_flat, mod, route_flat, ln_g.reshape(1, d), ln_b.reshape(1, d), ys)


def _moe(x2, mod, route, wegu_bf, wed_bf, ln_g, ln_b, alpha):
    b, s, d = x2.shape
    t = b * s
    tm = 256
    tr = min(256, s)
    n_tiles = (t * TOP_K) // tm + N_EXPERTS
    r = route.reshape(t, LANES)
    e_flat = r[:, :TOP_K].astype(jnp.int32).reshape(t * TOP_K)
    onehot = (e_flat[:, None] == jnp.arange(N_EXPERTS, dtype=jnp.int32)[None, :]).astype(jnp.int32)
    csum = jnp.cumsum(onehot, axis=0)
    rank = jnp.sum(onehot * (csum - 1), axis=1)
    counts = csum[-1]
    padded = ((counts + tm - 1) // tm) * tm
    ends = jnp.cumsum(padded)
    starts = ends - padded
    pos = (starts[e_flat] + rank).astype(jnp.int32)
    tile_start = jnp.arange(n_tiles, dtype=jnp.int32) * tm
    tile_expert = jnp.minimum(jnp.sum((tile_start[:, None] >= ends[None, :]).astype(jnp.int32), axis=1),
                              N_EXPERTS - 1).astype(jnp.int32)
    n_valid = (ends[-1:] // tm).astype(jnp.int32)
    spare = n_valid[0] + jnp.arange(N_EXPERTS, dtype=jnp.int32)
    zfill = jnp.stack([
        jnp.concatenate([jnp.maximum(ends - tm, 0), jnp.minimum(spare, n_tiles - 1) * tm]),
        jnp.concatenate([padded > 0, spare < n_tiles]).astype(jnp.int32),
    ]).astype(jnp.int32)
    pos3 = pos.reshape(t // tr, 1, TOP_K * tr)
    x2_flat = x2.reshape(t, d)
    assert tr == tm, "the dispatch kernel zero-fills one expert row tile from one token tile buffer"
    xs = _dispatch_call(x2_flat, mod, pos3, zfill, n_tiles * tm, s // tr)
    ys = _expert_call(tile_expert, n_valid, xs, wegu_bf, wed_bf, tm)
    out = _combine_call(x2_flat, mod, r, pos3, ys, ln_g, ln_b, s // tr, alpha)
    return out.reshape(b, s, d)


def kernel(x_prompt, x_sample, c_prompt, c_sample, w_ada, b_ada, ln_g, ln_b, w_qkv, diff_lambda,
           subln_g, w_o_attn, w_uv, b_uv, ln_v_g, ln_v_b, w_s, b_s, w_o_sgu, w_ffn_gu, w_ffn_down,
           w_router, w_e_gu, w_e_down):
    depth = w_ada.shape[0]
    d = x_prompt.shape[-1]
    alpha = (2 * depth) ** 0.25
    bp = c_prompt.shape[0]
    mods = _ada_call(jnp.concatenate([c_prompt, c_sample], axis=0), w_ada, b_ada)

    w_qkv_bf = w_qkv.astype(BF16)
    w_o_attn_bf = w_o_attn.astype(BF16)
    w_ffn_gu_bf = w_ffn_gu.astype(BF16)
    w_ffn_down_bf = w_ffn_down.astype(BF16)
    w_uv_bf = w_uv.astype(BF16)
    w_s_bf = w_s.astype(BF16)
    w_o_sgu_bf = w_o_sgu.astype(BF16)
    w_e_gu_bf = w_e_gu.astype(BF16)
    w_e_down_bf = w_e_down.astype(BF16)

    def trunk(x, r0, r1):
        for i in range(depth):
            j = i // 2
            mod = mods[i, r0:r1].reshape(r1 - r0, 6, d)
            if i % 2 == 0:
                lambda_init = 0.8 - 0.6 * math.exp(-0.3 * i)
                q2, k, v = _qkv_call(x, mod, w_qkv_bf[j])
                x = _attn_call(q2, k, v, x, mod, diff_lambda[j], subln_g[j], w_o_attn_bf[j],
                               ln_g[i, 0], ln_b[i, 0], lambda_init, alpha)
                x = _ffn_call(x, mod, w_ffn_gu_bf[j], w_ffn_down_bf[j], ln_g[i, 1], ln_b[i, 1], alpha)
            else:
                x, route = _sgu_call(x, mod, w_uv_bf[j], b_uv[j], ln_v_g[j], ln_v_b[j], w_s_bf[j], b_s[j],
                                     w_o_sgu_bf[j], ln_g[i, 0], ln_b[i, 0], w_router[j], alpha)
                x = _moe(x, mod, route, w_e_gu_bf[j], w_e_down_bf[j], ln_g[i, 1], ln_b[i, 1], alpha)
        return x

    y_prompt = trunk(x_prompt, 0, bp)
    y_sample = trunk(x_sample, bp, bp + c_sample.shape[0])
    return (y_prompt, y_sample)
```

```python
import functools
import math

import jax
import jax.numpy as jnp
from jax import lax
from jax.experimental import pallas as pl
from jax.experimental.pallas import tpu as pltpu

F32 = jnp.float32
BF16 = jnp.bfloat16

LANES = 128
N_HEADS = 8
HEAD_DIM = 64
V_DIM = 2 * HEAD_DIM
ROPE_THETA = 10000.0
SGU_CHUNK = 128
SGU_GROUPS = 8
N_EXPERTS = 8
TOP_K = 2
LN_EPS = 1e-5
VMEM_LIMIT_BYTES = 56 * 1024 * 1024

HIGHEST = lax.Precision.HIGHEST
LOG2_E = 1.4426950408889634
ATTN_KEY_CHUNK = 256


def _cparams(n_axes):
    return pltpu.CompilerParams(
        dimension_semantics=("arbitrary",) * n_axes,
        vmem_limit_bytes=VMEM_LIMIT_BYTES,
    )


def _ln(y, g, b):
    mu = jnp.mean(y, axis=-1, keepdims=True)
    yc = y - mu
    var = jnp.mean(yc * yc, axis=-1, keepdims=True)
    return yc * lax.rsqrt(var + LN_EPS) * g + b


def _ff_chunks(dff, chunk=512):
    out, c0 = [], 0
    while c0 < dff:
        c1 = min(c0 + chunk, dff)
        out.append((c0, c1))
        c0 = c1
    return out


def _swiglu(h, wgu_at, wd_at, dff):
    f = None
    for c0, c1 in _ff_chunks(dff):
        g = jnp.dot(h, wgu_at(c0, c1), preferred_element_type=F32)
        u = jnp.dot(h, wgu_at(dff + c0, dff + c1), preferred_element_type=F32)
        a = (g * jax.nn.sigmoid(g) * u).astype(BF16)
        p = jnp.dot(a, wd_at(c0, c1), preferred_element_type=F32)
        f = p if f is None else f + p
    return f


def _ada_kernel(c_ref, w_ref, b_ref, o_ref):
    c = c_ref[...]
    s = c * jax.nn.sigmoid(c)
    o_ref[0] = jnp.dot(s, w_ref[0], preferred_element_type=F32, precision=HIGHEST) + b_ref[0]


def _ada_call(c_all, w_ada, b_ada):
    depth, d, n6 = w_ada.shape
    nb = c_all.shape[0]
    tn = 1536
    return pl.pallas_call(
        _ada_kernel,
        grid=(depth, n6 // tn),
        in_specs=[
            pl.BlockSpec((nb, d), lambda l, j: (0, 0)),
            pl.BlockSpec((1, d, tn), lambda l, j: (l, 0, j)),
            pl.BlockSpec((1, 1, tn), lambda l, j: (l, 0, j)),
        ],
        out_specs=pl.BlockSpec((1, nb, tn), lambda l, j: (l, 0, j)),
        out_shape=jax.ShapeDtypeStruct((depth, nb, n6), F32),
        compiler_params=_cparams(2),
        name="ada",
    )(c_all, w_ada, b_ada.reshape(depth, 1, n6))


def _qkv_kernel(x_ref, mod_ref, w_ref, cq_ref, sq_ref, ck_ref, sk_ref, q_ref, k_ref, v_ref):
    d = x_ref.shape[-1]
    ts = x_ref.shape[1]
    x = x_ref[0]
    h = (x * (1.0 + mod_ref[0, 1:2, :]) + mod_ref[0, 0:1, :]).astype(BF16)
    qkv = jnp.dot(h, w_ref[...], preferred_element_type=F32)
    lane = lax.broadcasted_iota(jnp.int32, (ts, LANES), 1)
    first_half = (lane & (HEAD_DIM // 2)) == 0
    is_map1 = lane < HEAD_DIM
    cq, sq, ck, sk = cq_ref[...], sq_ref[...], ck_ref[...], sk_ref[...]

    def rope(t, cos, sin_signed):
        r = jnp.where(first_half, pltpu.roll(t, LANES - HEAD_DIM // 2, 1), pltpu.roll(t, HEAD_DIM // 2, 1))
        return t * cos + r * sin_signed

    for j in range(N_HEADS):
        qr = rope(qkv[:, j * LANES:(j + 1) * LANES], cq, sq)
        q_ref[0, 0, j] = jnp.where(is_map1, qr, 0.0).astype(BF16)
        q_ref[0, 1, j] = jnp.where(is_map1, 0.0, qr).astype(BF16)
        k_ref[0, j] = rope(qkv[:, d + j * LANES:d + (j + 1) * LANES], ck, sk).astype(BF16)
        v_ref[0, j] = qkv[:, 2 * d + j * LANES:2 * d + (j + 1) * LANES].astype(BF16)


def _rope_tables(seq):
    half = HEAD_DIM // 2
    inv = 1.0 / (ROPE_THETA ** (jnp.arange(0, HEAD_DIM, 2, dtype=F32) / HEAD_DIM))
    ang = jnp.arange(seq, dtype=F32)[:, None] * inv[None, :]
    cos = jnp.tile(jnp.cos(ang), (1, 2 * LANES // HEAD_DIM))
    sin = jnp.sin(ang)
    sin_signed = jnp.tile(jnp.concatenate([-sin, sin], axis=-1), (1, LANES // HEAD_DIM))
    del half
    scale = HEAD_DIM ** -0.5 * LOG2_E
    return cos * scale, sin_signed * scale, cos, sin_signed


def _qkv_call(x, mod, w_qkv_bf):
    b, s, d = x.shape
    ts = min(512, s)
    cq, sq, ck, sk = _rope_tables(s)
    tab = pl.BlockSpec((ts, LANES), lambda bi, si: (si, 0))
    return pl.pallas_call(
        _qkv_kernel,
        grid=(b, s // ts),
        in_specs=[
            pl.BlockSpec((1, ts, d), lambda bi, si: (bi, si, 0)),
            pl.BlockSpec((1, 6, d), lambda bi, si: (bi, 0, 0)),
            pl.BlockSpec((d, 3 * d), lambda bi, si: (0, 0)),
            tab, tab, tab, tab,
        ],
        out_specs=[
            pl.BlockSpec((1, 2, N_HEADS, ts, LANES), lambda bi, si: (bi, 0, 0, si, 0)),
            pl.BlockSpec((1, N_HEADS, ts, LANES), lambda bi, si: (bi, 0, si, 0)),
            pl.BlockSpec((1, N_HEADS, ts, LANES), lambda bi, si: (bi, 0, si, 0)),
        ],
        out_shape=[
            jax.ShapeDtypeStruct((b, 2, N_HEADS, s, LANES), BF16),
            jax.ShapeDtypeStruct((b, N_HEADS, s, LANES), BF16),
            jax.ShapeDtypeStruct((b, N_HEADS, s, LANES), BF16),
        ],
        compiler_params=_cparams(2),
        name="qkv_rope",
    )(x, mod, w_qkv_bf, cq, sq, ck, sk)


def _attn_kernel(q_ref, k_ref, v_ref, x_ref, mod_ref, lam_ref, sg_ref, wo_ref, lng_ref, lnb_ref,
                 o_ref, s_buf0, s_buf1, e_buf0, e_buf1, *, lambda_init, alpha):
    s_buf = (s_buf0, s_buf1)
    e_buf = (e_buf0, e_buf1)
    tq = x_ref.shape[1]
    ck = ATTN_KEY_CHUNK
    n_chunks = k_ref.shape[2] // ck
    lam = lam_ref[...]
    lam_full = (jnp.exp(jnp.sum(lam[0:1] * lam[1:2], keepdims=True))
                - jnp.exp(jnp.sum(lam[2:3] * lam[3:4], keepdims=True)) + lambda_init)

    def fold_lanes(t, op):
        out = t[:, :LANES]
        for c in range(1, ck // LANES):
            out = op(out, t[:, c * LANES:(c + 1) * LANES])
        return out

    qs, mpart, lpart, acc, m_row, r_row, inv_row = {}, {}, {}, {}, {}, {}, {}
    heads = []
    for i in range(-1, N_HEADS + 1):
        ha, hb, hc = i + 1, i, i - 1
        if 0 <= hb < N_HEADS:
            m_row[hb] = jnp.max(mpart.pop(hb), axis=-1, keepdims=True)
        if 0 <= hc < N_HEADS:
            l = jnp.sum(lpart.pop(hc), axis=-1, keepdims=True)
            r_row[hc] = lam_full * l[:tq] / l[tq:]
            inv_row[hc] = 1.0 / l[:tq]
        if ha < N_HEADS:
            qs[ha] = q_ref[0, :, ha].reshape(2 * tq, LANES)
        for j in range(n_chunks):
            cols = slice(j * ck, (j + 1) * ck)
            if ha < N_HEADS:
                s = lax.dot_general(qs[ha], k_ref[0, ha, cols, :], (((1,), (1,)), ((), ())),
                                    preferred_element_type=F32)
                s_buf[ha % 2][:, cols] = s
                sm = fold_lanes(s, jnp.maximum)
                mpart[ha] = sm if j == 0 else jnp.maximum(mpart[ha], sm)
            if 0 <= hb < N_HEADS:
                e = jnp.exp2(s_buf[hb % 2][:, cols] - m_row[hb])
                e_buf[hb % 2][:, cols] = e
                es = fold_lanes(e, jnp.add)
                lpart[hb] = es if j == 0 else lpart[hb] + es
            if 0 <= hc < N_HEADS:
                a = (e_buf[hc % 2][:tq, cols] - e_buf[hc % 2][tq:, cols] * r_row[hc]).astype(BF16)
                p = jnp.dot(a, v_ref[0, hc, cols, :], preferred_element_type=F32)
                acc[hc] = p if j == 0 else acc[hc] + p
        if 0 <= hc < N_HEADS:
            oh = acc.pop(hc) * inv_row[hc]
            ms = jnp.mean(oh * oh, axis=-1, keepdims=True)
            oh = oh * lax.rsqrt(ms + LN_EPS) * sg_ref[...] * (1.0 - lambda_init)
            heads.append(oh.astype(BF16))
    o = jnp.concatenate(heads, axis=1)
    mix = jnp.dot(o, wo_ref[...], preferred_element_type=F32)
    y = alpha * x_ref[0] + (1.0 + mod_ref[0, 2:3, :]) * mix
    o_ref[0] = _ln(y, lng_ref[...], lnb_ref[...])


def _attn_call(q2, k, v, x, mod, lam, subln_g, wo_bf, ln_g, ln_b, lambda_init, alpha):
    b, s, d = x.shape
    tq = min(256, s)
    const2 = lambda bi, qi: (0, 0)
    return pl.pallas_call(
        functools.partial(_attn_kernel, lambda_init=lambda_init, alpha=alpha),
        grid=(b, s // tq),
        in_specs=[
            pl.BlockSpec((1, 2, N_HEADS, tq, LANES), lambda bi, qi: (bi, 0, 0, qi, 0)),
            pl.BlockSpec((1, N_HEADS, s, LANES), lambda bi, qi: (bi, 0, 0, 0)),
            pl.BlockSpec((1, N_HEADS, s, LANES), lambda bi, qi: (bi, 0, 0, 0)),
            pl.BlockSpec((1, tq, d), lambda bi, qi: (bi, qi, 0)),
            pl.BlockSpec((1, 6, d), lambda bi, qi: (bi, 0, 0)),
            pl.BlockSpec((4, HEAD_DIM), const2),
            pl.BlockSpec((1, V_DIM), const2),
            pl.BlockSpec((d, d), const2),
            pl.BlockSpec((1, d), const2),
            pl.BlockSpec((1, d), const2),
        ],
        out_specs=pl.BlockSpec((1, tq, d), lambda bi, qi: (bi, qi, 0)),
        out_shape=jax.ShapeDtypeStruct((b, s, d), F32),
        scratch_shapes=[pltpu.VMEM((2 * tq, s), F32)] * 4,
        compiler_params=_cparams(2),
        name="diff_attn",
    )(q2, k, v, x, mod, lam, subln_g.reshape(1, V_DIM), wo_bf, ln_g.reshape(1, d), ln_b.reshape(1, d))


def _ffn_kernel(x_ref, mod_ref, wgu_ref, wd_ref, lng_ref, lnb_ref, o_ref, *, alpha):
    dff = wd_ref.shape[0]
    x = x_ref[0]
    h = (x * (1.0 + mod_ref[0, 4:5, :]) + mod_ref[0, 3:4, :]).astype(BF16)
    f = _swiglu(h, lambda c0, c1: wgu_ref[:, c0:c1], lambda c0, c1: wd_ref[c0:c1, :], dff)
    y = alpha * x + (1.0 + mod_ref[0, 5:6, :]) * f
    o_ref[0] = _ln(y, lng_ref[...], lnb_ref[...])


def _ffn_call(x, mod, wgu_bf, wd_bf, ln_g, ln_b, alpha):
    b, s, d = x.shape
    dff = wd_bf.shape[0]
    tm = min(512, s)
    const2 = lambda bi, si: (0, 0)
    return pl.pallas_call(
        functools.partial(_ffn_kernel, alpha=alpha),
        grid=(b, s // tm),
        in_specs=[
            pl.BlockSpec((1, tm, d), lambda bi, si: (bi, si, 0)),
            pl.BlockSpec((1, 6, d), lambda bi, si: (bi, 0, 0)),
            pl.BlockSpec((d, 2 * dff), const2, pipeline_mode=pl.Buffered(1)),
            pl.BlockSpec((dff, d), const2, pipeline_mode=pl.Buffered(1)),
            pl.BlockSpec((1, d), const2),
            pl.BlockSpec((1, d), const2),
        ],
        out_specs=pl.BlockSpec((1, tm, d), lambda bi, si: (bi, si, 0)),
        out_shape=jax.ShapeDtypeStruct((b, s, d), F32),
        compiler_params=_cparams(2),
        name="dense_swiglu",
    )(x, mod, wgu_bf, wd_bf, ln_g.reshape(1, d), ln_b.reshape(1, d))


def _sgu_kernel(x_ref, mod_ref, wuv_ref, buv_ref, lvg_ref, lvb_ref, ws_ref, bs_ref, wo_ref,
                lng_ref, lnb_ref, wr_ref, x2_ref, route_ref, *, alpha, n_sub):
    tm = x_ref.shape[1]
    sub = tm // n_sub
    width = wo_ref.shape[0]
    blocks = [(i * sub, (i + 1) * sub) for i in range(n_sub)]
    xs = [x_ref[0, r0:r1, :] for r0, r1 in blocks]
    hs = [(x * (1.0 + mod_ref[0, 1:2, :]) + mod_ref[0, 0:1, :]).astype(BF16) for x in xs]
    zs = [jnp.dot(h, wuv_ref[...], preferred_element_type=F32) + buv_ref[...] for h in hs]
    zs = [0.5 * z * (1.0 + lax.erf(z * (2.0 ** -0.5))) for z in zs]
    us = [z[:, :width] for z in zs]
    vs = [_ln(z[:, width:], lvg_ref[...], lvb_ref[...]).astype(BF16) for z in zs]

    def spatial(u, v):
        rows = []
        for n in range(sub // SGU_CHUNK):
            r0, r1 = n * SGU_CHUNK, (n + 1) * SGU_CHUNK
            cols = []
            for g in range(SGU_GROUPS):
                c0, c1 = g * LANES, (g + 1) * LANES
                sp = jnp.dot(ws_ref[g], v[r0:r1, c0:c1], preferred_element_type=F32) + bs_ref[g]
                cols.append(u[r0:r1, c0:c1] * sp)
            rows.append(jnp.concatenate(cols, axis=1))
        return jnp.concatenate(rows, axis=0).astype(BF16)

    gated = [spatial(u, v) for u, v in zip(us, vs)]
    mixes = [jnp.dot(g, wo_ref[...], preferred_element_type=F32) for g in gated]
    x2s = [_ln(alpha * x + (1.0 + mod_ref[0, 2:3, :]) * mix, lng_ref[...], lnb_ref[...])
           for x, mix in zip(xs, mixes)]
    for (r0, r1), x2 in zip(blocks, x2s):
        x2_ref[0, r0:r1, :] = x2

    def split_hi_lo(x2):
        h2 = x2 * (1.0 + mod_ref[0, 4:5, :]) + mod_ref[0, 3:4, :]
        h_hi = h2.astype(BF16)
        h_lo = (h2 - h_hi.astype(F32)).astype(BF16)
        return jnp.concatenate([h_hi, h_lo], axis=0)

    prods = [jnp.dot(split_hi_lo(x2), wr_ref[...], preferred_element_type=F32) for x2 in x2s]
    lane = lax.broadcasted_iota(jnp.int32, (sub, LANES), 1)
    neg = jnp.float32(-jnp.inf)
    for (r0, r1), p in zip(blocks, prods):
        both = p[:sub] + p[sub:]
        logits = both + pltpu.roll(both, LANES - N_EXPERTS, 1)
        logits = jnp.where(lane < N_EXPERTS, logits, neg)
        m1 = jnp.max(logits, axis=-1, keepdims=True)
        i1 = jnp.min(jnp.where(logits == m1, lane, LANES), axis=-1, keepdims=True)
        rest = jnp.where(lane == i1, neg, logits)
        m2 = jnp.max(rest, axis=-1, keepdims=True)
        i2 = jnp.min(jnp.where(rest == m2, lane, LANES), axis=-1, keepdims=True)
        t = jnp.exp(m2 - m1)
        w1 = 1.0 / (1.0 + t)
        w2 = t * w1
        route_ref[0, r0:r1, :] = jnp.where(
            lane == 0, i1.astype(F32),
            jnp.where(lane == 1, i2.astype(F32), jnp.where(lane == 2, w1, jnp.where(lane == 3, w2, 0.0))))


def _sgu_call(x, mod, wuv_bf, b_uv, ln_v_g, ln_v_b, ws_bf, b_s, wo_bf, ln_g, ln_b, w_router, alpha):
    b, s, d = x.shape
    width = wo_bf.shape[0]
    tm = min(512, s)
    n_sub = 2 if tm % (2 * SGU_CHUNK) == 0 else 1
    w_hi = w_router.astype(BF16)
    w_lo = (w_router - w_hi.astype(F32)).astype(BF16)
    wr_pad = (jnp.zeros((d, LANES), BF16).at[:, :N_EXPERTS].set(w_hi)
              .at[:, N_EXPERTS:2 * N_EXPERTS].set(w_lo))
    const2 = lambda bi, si: (0, 0)
    const3 = lambda bi, si: (0, 0, 0)
    return pl.pallas_call(
        functools.partial(_sgu_kernel, alpha=alpha, n_sub=n_sub),
        grid=(b, s // tm),
        in_specs=[
            pl.BlockSpec((1, tm, d), lambda bi, si: (bi, si, 0)),
            pl.BlockSpec((1, 6, d), lambda bi, si: (bi, 0, 0)),
            pl.BlockSpec((d, 2 * width), const2),
            pl.BlockSpec((1, 2 * width), const2),
            pl.BlockSpec((1, width), const2),
            pl.BlockSpec((1, width), const2),
            pl.BlockSpec((SGU_GROUPS, SGU_CHUNK, SGU_CHUNK), const3),
            pl.BlockSpec((SGU_GROUPS, SGU_CHUNK, 1), const3),
            pl.BlockSpec((width, d), const2),
            pl.BlockSpec((1, d), const2),
            pl.BlockSpec((1, d), const2),
            pl.BlockSpec((d, LANES), const2),
        ],
        out_specs=[
            pl.BlockSpec((1, tm, d), lambda bi, si: (bi, si, 0)),
            pl.BlockSpec((1, tm, LANES), lambda bi, si: (bi, si, 0)),
        ],
        out_shape=[
            jax.ShapeDtypeStruct((b, s, d), F32),
            jax.ShapeDtypeStruct((b, s, LANES), F32),
        ],
        compiler_params=_cparams(2),
        name="sgu_router",
    )(x, mod, wuv_bf, b_uv.reshape(1, -1), ln_v_g.reshape(1, -1), ln_v_b.reshape(1, -1), ws_bf,
      b_s.reshape(SGU_GROUPS, SGU_CHUNK, 1), wo_bf, ln_g.reshape(1, d), ln_b.reshape(1, d), wr_pad)


MOE_BLOCK = 256
MOE_TILE = 256
SEG_ALIGN = 8
SEG_BITS = 6
MOE_CAP = 640


def _segment_copies(info_ref, make_copy, action):
    for e in range(N_EXPERTS):
        local = info_ref[0, 0, e]
        n_units = lax.shift_right_logical(info_ref[0, 0, N_EXPERTS + e], 3)
        glob = info_ref[0, 0, 2 * N_EXPERTS + e]
        off = jnp.int32(0)
        for j in reversed(range(SEG_BITS)):
            rows = SEG_ALIGN << j
            bit = lax.shift_right_logical(n_units, j) & 1

            @pl.when(bit == 1)
            def _():
                action(make_copy(pl.multiple_of(local + off, SEG_ALIGN), pl.multiple_of(glob + off, SEG_ALIGN), rows))

            off = off + bit * rows


def _dispatch_kernel(info_ref, infop_ref, zfill_ref, lp_ref, x_ref, mod_ref, xs_ref, sbuf, sem):
    i = pl.program_id(0)
    n = pl.num_programs(0)
    tb, d = x_ref.shape
    slot = i % 2

    @pl.when(i == 0)
    def _():
        sbuf[1, 0:MOE_TILE, :] = jnp.zeros((MOE_TILE, d), F32)

        def zero_tile(t):
            start = pl.multiple_of(t * MOE_TILE, MOE_TILE)
            return pltpu.make_async_copy(sbuf.at[1, pl.ds(0, MOE_TILE)], xs_ref.at[pl.ds(start, MOE_TILE)], sem.at[2])

        for e in range(N_EXPERTS):
            @pl.when(zfill_ref[e] >= 0)
            def _():
                zero_tile(zfill_ref[e]).start()
        for e in range(N_EXPERTS):
            @pl.when(zfill_ref[e] >= 0)
            def _():
                zero_tile(zfill_ref[e]).wait()

        def spare(t, carry):
            cp = zero_tile(t)
            cp.start()
            cp.wait()
            return carry

        lax.fori_loop(zfill_ref[N_EXPERTS], zfill_ref[N_EXPERTS + 1], spare, 0)

    h2 = (x_ref[...] * (1.0 + mod_ref[0, 4:5, :]) + mod_ref[0, 3:4, :]).astype(BF16)
    row = lax.broadcasted_iota(jnp.int32, (MOE_CAP, tb), 0)
    perm = jnp.logical_or(row == lp_ref[0, 0:1, :], row == lp_ref[0, 1:2, :]).astype(BF16)
    sbuf[slot] = jnp.dot(perm, h2, preferred_element_type=F32)

    def copy_out(s):
        return lambda local, glob, rows: pltpu.make_async_copy(
            sbuf.at[s, pl.ds(local, rows)], xs_ref.at[pl.ds(glob, rows)], sem.at[s])

    _segment_copies(info_ref, copy_out(slot), lambda cp: cp.start())

    @pl.when(i >= 1)
    def _():
        _segment_copies(infop_ref, copy_out(1 - slot), lambda cp: cp.wait())

    @pl.when(i == n - 1)
    def _():
        _segment_copies(info_ref, copy_out(slot), lambda cp: cp.wait())


def _dispatch_call(x2_flat, mod, info, zfill, lp_rows, n_rows, blocks_per_batch):
    t, d = x2_flat.shape
    tb = MOE_BLOCK
    n_info = info.shape[-1]
    return pl.pallas_call(
        _dispatch_kernel,
        grid=(t // tb,),
        in_specs=[
            pl.BlockSpec((1, 1, n_info), lambda i: (i, 0, 0), memory_space=pltpu.SMEM),
            pl.BlockSpec((1, 1, n_info), lambda i: (jnp.maximum(i - 1, 0), 0, 0), memory_space=pltpu.SMEM),
            pl.BlockSpec(memory_space=pltpu.SMEM),
            pl.BlockSpec((1, TOP_K, tb), lambda i: (i, 0, 0)),
            pl.BlockSpec((tb, d), lambda i: (i, 0)),
            pl.BlockSpec((1, 6, d), lambda i: (i // blocks_per_batch, 0, 0)),
        ],
        out_specs=pl.BlockSpec(memory_space=pl.ANY),
        out_shape=jax.ShapeDtypeStruct((n_rows, d), F32),
        scratch_shapes=[pltpu.VMEM((2, MOE_CAP, d), F32), pltpu.SemaphoreType.DMA((3,))],
        compiler_params=_cparams(1),
        name="moe_dispatch",
    )(info, info, zfill, lp_rows, x2_flat, mod)


def _expert_kernel(te_ref, nv_ref, xs_ref, wgu_ref, wd_ref, ys_ref):
    del te_ref
    t = pl.program_id(0)
    dff = wd_ref.shape[1]

    @pl.when(t < nv_ref[0])
    def _():
        h = xs_ref[...].astype(BF16)
        ys_ref[...] = _swiglu(h, lambda c0, c1: wgu_ref[0, :, c0:c1], lambda c0, c1: wd_ref[0, c0:c1, :], dff)

    @pl.when(t >= nv_ref[0])
    def _():
        ys_ref[...] = jnp.zeros_like(ys_ref)


def _expert_call(tile_expert, n_valid, xs, wegu_bf, wed_bf, tm):
    n_rows, d = xs.shape
    dff = wed_bf.shape[1]
    grid_spec = pltpu.PrefetchScalarGridSpec(
        num_scalar_prefetch=2,
        grid=(n_rows // tm,),
        in_specs=[
            pl.BlockSpec((tm, d), lambda t, te, nv: (jnp.minimum(t, nv[0] - 1), 0)),
            pl.BlockSpec((1, d, 2 * dff), lambda t, te, nv: (te[t], 0, 0)),
            pl.BlockSpec((1, dff, d), lambda t, te, nv: (te[t], 0, 0)),
        ],
        out_specs=pl.BlockSpec((tm, d), lambda t, te, nv: (t, 0)),
    )
    return pl.pallas_call(
        _expert_kernel,
        grid_spec=grid_spec,
        out_shape=jax.ShapeDtypeStruct((n_rows, d), F32),
        compiler_params=_cparams(1),
        name="moe_experts",
    )(tile_expert, n_valid, xs, wegu_bf, wed_bf)


def _combine_kernel(info_ref, infon_ref, lp_ref, x_ref, mod_ref, route_ref, lng_ref, lnb_ref, ys_ref, o_ref,
                    ybuf, sem, *, alpha):
    i = pl.program_id(0)
    n = pl.num_programs(0)
    tb = x_ref.shape[0]
    slot = i % 2

    def copy_in(s):
        return lambda local, glob, rows: pltpu.make_async_copy(
            ys_ref.at[pl.ds(glob, rows)], ybuf.at[s, pl.ds(local, rows)], sem.at[s])

    @pl.when(i == 0)
    def _():
        ybuf[...] = jnp.zeros(ybuf.shape, F32)
        _segment_copies(info_ref, copy_in(0), lambda cp: cp.start())

    @pl.when(i + 1 < n)
    def _():
        _segment_copies(infon_ref, copy_in(1 - slot), lambda cp: cp.start())

    _segment_copies(info_ref, copy_in(slot), lambda cp: cp.wait())

    yb = ybuf[slot].astype(BF16)
    lane = lax.broadcasted_iota(jnp.int32, (tb, MOE_CAP), 1)
    pick = jnp.concatenate([lane == lp_ref[:, 0:1], lane == lp_ref[:, 1:2]], axis=0).astype(BF16)
    ysel = jnp.dot(pick, yb, preferred_element_type=F32)
    x = x_ref[...]
    f = route_ref[:, 2:3] * ysel[:tb] + route_ref[:, 3:4] * ysel[tb:]
    y = alpha * x + (1.0 + mod_ref[0, 5:6, :]) * f
    o_ref[...] = _ln(y, lng_ref[...], lnb_ref[...])


def _combine_call(x2_flat, mod, route_flat, info, lp_cols, ys, ln_g, ln_b, blocks_per_batch, alpha):
    t, d = x2_flat.shape
    tb = MOE_BLOCK
    n = t // tb
    n_info = info.shape[-1]
    return pl.pallas_call(
        functools.partial(_combine_kernel, alpha=alpha),
        grid=(n,),
        in_specs=[
            pl.BlockSpec((1, 1, n_info), lambda i: (i, 0, 0), memory_space=pltpu.SMEM),
            pl.BlockSpec((1, 1, n_info), lambda i: (jnp.minimum(i + 1, n - 1), 0, 0), memory_space=pltpu.SMEM),
            pl.BlockSpec((tb, TOP_K), lambda i: (i, 0)),
            pl.BlockSpec((tb, d), lambda i: (i, 0)),
            pl.BlockSpec((1, 6, d), lambda i: (i // blocks_per_batch, 0, 0)),
            pl.BlockSpec((tb, LANES), lambda i: (i, 0)),
            pl.BlockSpec((1, d), lambda i: (0, 0)),
            pl.BlockSpec((1, d), lambda i: (0, 0)),
            pl.BlockSpec(memory_space=pl.ANY),
        ],
        out_specs=pl.BlockSpec((tb, d), lambda i: (i, 0)),
        out_shape=jax.ShapeDtypeStruct((t, d), F32),
        scratch_shapes=[pltpu.VMEM((2, MOE_CAP, d), F32), pltpu.SemaphoreType.DMA((2,))],
        compiler_params=_cparams(1),
        name="moe_combine",
    )(info, info, lp_cols, x2_flat, mod, route_flat, ln_g.reshape(1, d), ln_b.reshape(1, d), ys)


def _moe(x2, mod, route, wegu_bf, wed_bf, ln_g, ln_b, alpha):
    b, s, d = x2.shape
    t = b * s
    tb, tm = MOE_BLOCK, MOE_TILE
    assert s % tb == 0 and TOP_K * tb + N_EXPERTS * (SEG_ALIGN - 1) <= MOE_CAP
    assert tb <= SEG_ALIGN << (SEG_BITS - 1) and SEG_ALIGN == 8
    nb = t // tb
    r = route.reshape(t, LANES)
    eb = r[:, :TOP_K].astype(jnp.int32).reshape(nb, tb * TOP_K)
    onehot = (eb[:, :, None] == jnp.arange(N_EXPERTS, dtype=jnp.int32)).astype(jnp.int32)
    csum = jnp.cumsum(onehot, axis=1)
    rank = jnp.sum(onehot * (csum - 1), axis=2)
    seg_len = ((csum[:, -1, :] + SEG_ALIGN - 1) // SEG_ALIGN) * SEG_ALIGN
    seg_local = jnp.cumsum(seg_len, axis=1) - seg_len
    lp = jnp.take_along_axis(seg_local, eb, axis=1) + rank
    rows_e = jnp.sum(seg_len, axis=0)
    padded = ((rows_e + tm - 1) // tm) * tm
    ends = jnp.cumsum(padded)
    starts = ends - padded
    seg_glob = starts[None, :] + jnp.cumsum(seg_len, axis=0) - seg_len
    info = jnp.concatenate([seg_local, seg_len, seg_glob], axis=1).astype(jnp.int32).reshape(nb, 1, 3 * N_EXPERTS)
    lp_rows = lp.reshape(nb, tb, TOP_K).transpose(0, 2, 1).astype(jnp.int32)
    lp_cols = lp.reshape(t, TOP_K).astype(jnp.int32)
    n_tiles = (t * TOP_K + nb * N_EXPERTS * (SEG_ALIGN - 1) + tm - 1) // tm + N_EXPERTS
    tile_start = jnp.arange(n_tiles, dtype=jnp.int32) * tm
    tile_expert = jnp.minimum(jnp.sum((tile_start[:, None] >= ends[None, :]).astype(jnp.int32), axis=1),
                              N_EXPERTS - 1).astype(jnp.int32)
    n_valid = (ends[-1:] // tm).astype(jnp.int32)
    zfill = jnp.concatenate([jnp.where(padded > 0, ends // tm - 1, -1), n_valid,
                             jnp.full((1,), n_tiles)]).astype(jnp.int32)
    x2_flat = x2.reshape(t, d)
    xs = _dispatch_call(x2_flat, mod, info, zfill, lp_rows, n_tiles * tm, s // tb)
    ys = _expert_call(tile_expert, n_valid, xs, wegu_bf, wed_bf, tm)
    out = _combine_call(x2_flat, mod, r, info, lp_cols, ys, ln_g, ln_b, s // tb, alpha)
    return out.reshape(b, s, d)


def kernel(x_prompt, x_sample, c_prompt, c_sample, w_ada, b_ada, ln_g, ln_b, w_qkv, diff_lambda,
           subln_g, w_o_attn, w_uv, b_uv, ln_v_g, ln_v_b, w_s, b_s, w_o_sgu, w_ffn_gu, w_ffn_down,
           w_router, w_e_gu, w_e_down):
    depth = w_ada.shape[0]
    d = x_prompt.shape[-1]
    alpha = (2 * depth) ** 0.25
    bp = c_prompt.shape[0]
    mods = _ada_call(jnp.concatenate([c_prompt, c_sample], axis=0), w_ada, b_ada)

    w_qkv_bf = w_qkv.astype(BF16)
    w_o_attn_bf = w_o_attn.astype(BF16)
    w_ffn_gu_bf = w_ffn_gu.astype(BF16)
    w_ffn_down_bf = w_ffn_down.astype(BF16)
    w_uv_bf = w_uv.astype(BF16)
    w_s_bf = w_s.astype(BF16)
    w_o_sgu_bf = w_o_sgu.astype(BF16)
    w_e_gu_bf = w_e_gu.astype(BF16)
    w_e_down_bf = w_e_down.astype(BF16)

    def trunk(x, r0, r1):
        for i in range(depth):
            j = i // 2
            mod = mods[i, r0:r1].reshape(r1 - r0, 6, d)
            if i % 2 == 0:
                lambda_init = 0.8 - 0.6 * math.exp(-0.3 * i)
                q2, k, v = _qkv_call(x, mod, w_qkv_bf[j])
                x = _attn_call(q2, k, v, x, mod, diff_lambda[j], subln_g[j], w_o_attn_bf[j],
                               ln_g[i, 0], ln_b[i, 0], lambda_init, alpha)
                x = _ffn_call(x, mod, w_ffn_gu_bf[j], w_ffn_down_bf[j], ln_g[i, 1], ln_b[i, 1], alpha)
            else:
                x, route = _sgu_call(x, mod, w_uv_bf[j], b_uv[j], ln_v_g[j], ln_v_b[j], w_s_bf[j], b_s[j],
                                     w_o_sgu_bf[j], ln_g[i, 0], ln_b[i, 0], w_router[j], alpha)
                x = _moe(x, mod, route, w_e_gu_bf[j], w_e_down_bf[j], ln_g[i, 1], ln_b[i, 1], alpha)
        return x

    y_prompt = trunk(x_prompt, 0, bp)
    y_sample = trunk(x_sample, bp, bp + c_sample.shape[0])
    return (y_prompt, y_sample)
```

```python
import functools
import math

import jax
import jax.numpy as jnp
from jax import lax
from jax.experimental import pallas as pl
from jax.experimental.pallas import tpu as pltpu

F32 = jnp.float32
BF16 = jnp.bfloat16

LANES = 128
N_HEADS = 8
HEAD_DIM = 64
V_DIM = 2 * HEAD_DIM
ROPE_THETA = 10000.0
SGU_CHUNK = 128
SGU_GROUPS = 8
N_EXPERTS = 8
TOP_K = 2
LN_EPS = 1e-5
VMEM_LIMIT_BYTES = 56 * 1024 * 1024

HIGHEST = lax.Precision.HIGHEST
LOG2_E = 1.4426950408889634
ATTN_KEY_CHUNK = 256


def _cparams(n_axes):
    return pltpu.CompilerParams(
        dimension_semantics=("arbitrary",) * n_axes,
        vmem_limit_bytes=VMEM_LIMIT_BYTES,
    )


def _ln(y, g, b):
    mu = jnp.mean(y, axis=-1, keepdims=True)
    yc = y - mu
    var = jnp.mean(yc * yc, axis=-1, keepdims=True)
    return yc * lax.rsqrt(var + LN_EPS) * g + b


def _ff_chunks(dff, chunk=512):
    out, c0 = [], 0
    while c0 < dff:
        c1 = min(c0 + chunk, dff)
        out.append((c0, c1))
        c0 = c1
    return out


def _swiglu(h, wgu_at, wd_at, dff):
    f = None
    for c0, c1 in _ff_chunks(dff):
        g = jnp.dot(h, wgu_at(c0, c1), preferred_element_type=F32)
        u = jnp.dot(h, wgu_at(dff + c0, dff + c1), preferred_element_type=F32)
        a = (g * jax.nn.sigmoid(g) * u).astype(BF16)
        p = jnp.dot(a, wd_at(c0, c1), preferred_element_type=F32)
        f = p if f is None else f + p
    return f


def _ada_kernel(c_ref, w_ref, b_ref, o_ref):
    c = c_ref[...]
    s = c * jax.nn.sigmoid(c)
    o_ref[0] = jnp.dot(s, w_ref[0], preferred_element_type=F32, precision=HIGHEST) + b_ref[0]


def _ada_call(c_all, w_ada, b_ada):
    depth, d, n6 = w_ada.shape
    nb = c_all.shape[0]
    tn = 1536
    return pl.pallas_call(
        _ada_kernel,
        grid=(depth, n6 // tn),
        in_specs=[
            pl.BlockSpec((nb, d), lambda l, j: (0, 0)),
            pl.BlockSpec((1, d, tn), lambda l, j: (l, 0, j)),
            pl.BlockSpec((1, 1, tn), lambda l, j: (l, 0, j)),
        ],
        out_specs=pl.BlockSpec((1, nb, tn), lambda l, j: (l, 0, j)),
        out_shape=jax.ShapeDtypeStruct((depth, nb, n6), F32),
        compiler_params=_cparams(2),
        name="ada",
    )(c_all, w_ada, b_ada.reshape(depth, 1, n6))


def _qkv_kernel(x_ref, mod_ref, w_ref, cq_ref, sq_ref, ck_ref, sk_ref, q_ref, k_ref, v_ref):
    d = x_ref.shape[-1]
    ts = x_ref.shape[1]
    x = x_ref[0]
    h = (x * (1.0 + mod_ref[0, 1:2, :]) + mod_ref[0, 0:1, :]).astype(BF16)
    qkv = jnp.dot(h, w_ref[...], preferred_element_type=F32)
    lane = lax.broadcasted_iota(jnp.int32, (ts, LANES), 1)
    first_half = (lane & (HEAD_DIM // 2)) == 0
    is_map1 = lane < HEAD_DIM
    cq, sq, ck, sk = cq_ref[...], sq_ref[...], ck_ref[...], sk_ref[...]

    def rope(t, cos, sin_signed):
        r = jnp.where(first_half, pltpu.roll(t, LANES - HEAD_DIM // 2, 1), pltpu.roll(t, HEAD_DIM // 2, 1))
        return t * cos + r * sin_signed

    for j in range(N_HEADS):
        qr = rope(qkv[:, j * LANES:(j + 1) * LANES], cq, sq)
        q_ref[0, 0, j] = jnp.where(is_map1, qr, 0.0).astype(BF16)
        q_ref[0, 1, j] = jnp.where(is_map1, 0.0, qr).astype(BF16)
        k_ref[0, j] = rope(qkv[:, d + j * LANES:d + (j + 1) * LANES], ck, sk).astype(BF16)
        v_ref[0, j] = qkv[:, 2 * d + j * LANES:2 * d + (j + 1) * LANES].astype(BF16)


def _rope_tables(seq):
    half = HEAD_DIM // 2
    inv = 1.0 / (ROPE_THETA ** (jnp.arange(0, HEAD_DIM, 2, dtype=F32) / HEAD_DIM))
    ang = jnp.arange(seq, dtype=F32)[:, None] * inv[None, :]
    cos = jnp.tile(jnp.cos(ang), (1, 2 * LANES // HEAD_DIM))
    sin = jnp.sin(ang)
    sin_signed = jnp.tile(jnp.concatenate([-sin, sin], axis=-1), (1, LANES // HEAD_DIM))
    del half
    scale = HEAD_DIM ** -0.5 * LOG2_E
    return cos * scale, sin_signed * scale, cos, sin_signed


def _qkv_call(x, mod, w_qkv_bf):
    b, s, d = x.shape
    ts = min(512, s)
    cq, sq, ck, sk = _rope_tables(s)
    tab = pl.BlockSpec((ts, LANES), lambda bi, si: (si, 0))
    return pl.pallas_call(
        _qkv_kernel,
        grid=(b, s // ts),
        in_specs=[
            pl.BlockSpec((1, ts, d), lambda bi, si: (bi, si, 0)),
            pl.BlockSpec((1, 6, d), lambda bi, si: (bi, 0, 0)),
            pl.BlockSpec((d, 3 * d), lambda bi, si: (0, 0)),
            tab, tab, tab, tab,
        ],
        out_specs=[
            pl.BlockSpec((1, 2, N_HEADS, ts, LANES), lambda bi, si: (bi, 0, 0, si, 0)),
            pl.BlockSpec((1, N_HEADS, ts, LANES), lambda bi, si: (bi, 0, si, 0)),
            pl.BlockSpec((1, N_HEADS, ts, LANES), lambda bi, si: (bi, 0, si, 0)),
        ],
        out_shape=[
            jax.ShapeDtypeStruct((b, 2, N_HEADS, s, LANES), BF16),
            jax.ShapeDtypeStruct((b, N_HEADS, s, LANES), BF16),
            jax.ShapeDtypeStruct((b, N_HEADS, s, LANES), BF16),
        ],
        compiler_params=_cparams(2),
        name="qkv_rope",
    )(x, mod, w_qkv_bf, cq, sq, ck, sk)


def _attn_kernel(q_ref, k_ref, v_ref, x_ref, mod_ref, lam_ref, sg_ref, wo_ref, lng_ref, lnb_ref,
                 o_ref, s_buf0, s_buf1, e_buf0, e_buf1, *, lambda_init, alpha):
    s_buf = (s_buf0, s_buf1)
    e_buf = (e_buf0, e_buf1)
    tq = x_ref.shape[1]
    ck = ATTN_KEY_CHUNK
    n_chunks = k_ref.shape[2] // ck
    lam = lam_ref[...]
    lam_full = (jnp.exp(jnp.sum(lam[0:1] * lam[1:2], keepdims=True))
                - jnp.exp(jnp.sum(lam[2:3] * lam[3:4], keepdims=True)) + lambda_init)

    def fold_lanes(t, op):
        out = t[:, :LANES]
        for c in range(1, ck // LANES):
            out = op(out, t[:, c * LANES:(c + 1) * LANES])
        return out

    qs, mpart, lpart, acc, m_row, r_row, inv_row = {}, {}, {}, {}, {}, {}, {}
    heads = []
    for i in range(-1, N_HEADS + 1):
        ha, hb, hc = i + 1, i, i - 1
        if 0 <= hb < N_HEADS:
            m_row[hb] = jnp.max(mpart.pop(hb), axis=-1, keepdims=True)
        if 0 <= hc < N_HEADS:
            l = jnp.sum(lpart.pop(hc), axis=-1, keepdims=True)
            r_row[hc] = lam_full * l[:tq] / l[tq:]
            inv_row[hc] = 1.0 / l[:tq]
        if ha < N_HEADS:
            qs[ha] = q_ref[0, :, ha].reshape(2 * tq, LANES)
        for j in range(n_chunks):
            cols = slice(j * ck, (j + 1) * ck)
            if ha < N_HEADS:
                s = lax.dot_general(qs[ha], k_ref[0, ha, cols, :], (((1,), (1,)), ((), ())),
                                    preferred_element_type=F32)
                s_buf[ha % 2][:, cols] = s
                sm = fold_lanes(s, jnp.maximum)
                mpart[ha] = sm if j == 0 else jnp.maximum(mpart[ha], sm)
            if 0 <= hb < N_HEADS:
                e = jnp.exp2(s_buf[hb % 2][:, cols] - m_row[hb])
                e_buf[hb % 2][:, cols] = e
                es = fold_lanes(e, jnp.add)
                lpart[hb] = es if j == 0 else lpart[hb] + es
            if 0 <= hc < N_HEADS:
                a = (e_buf[hc % 2][:tq, cols] - e_buf[hc % 2][tq:, cols] * r_row[hc]).astype(BF16)
                p = jnp.dot(a, v_ref[0, hc, cols, :], preferred_element_type=F32)
                acc[hc] = p if j == 0 else acc[hc] + p
        if 0 <= hc < N_HEADS:
            oh = acc.pop(hc) * inv_row[hc]
            ms = jnp.mean(oh * oh, axis=-1, keepdims=True)
            oh = oh * lax.rsqrt(ms + LN_EPS) * sg_ref[...] * (1.0 - lambda_init)
            heads.append(oh.astype(BF16))
    o = jnp.concatenate(heads, axis=1)
    mix = jnp.dot(o, wo_ref[...], preferred_element_type=F32)
    y = alpha * x_ref[0] + (1.0 + mod_ref[0, 2:3, :]) * mix
    o_ref[0] = _ln(y, lng_ref[...], lnb_ref[...])


def _attn_call(q2, k, v, x, mod, lam, subln_g, wo_bf, ln_g, ln_b, lambda_init, alpha):
    b, s, d = x.shape
    tq = min(256, s)
    const2 = lambda bi, qi: (0, 0)
    return pl.pallas_call(
        functools.partial(_attn_kernel, lambda_init=lambda_init, alpha=alpha),
        grid=(b, s // tq),
        in_specs=[
            pl.BlockSpec((1, 2, N_HEADS, tq, LANES), lambda bi, qi: (bi, 0, 0, qi, 0)),
            pl.BlockSpec((1, N_HEADS, s, LANES), lambda bi, qi: (bi, 0, 0, 0)),
            pl.BlockSpec((1, N_HEADS, s, LANES), lambda bi, qi: (bi, 0, 0, 0)),
            pl.BlockSpec((1, tq, d), lambda bi, qi: (bi, qi, 0)),
            pl.BlockSpec((1, 6, d), lambda bi, qi: (bi, 0, 0)),
            pl.BlockSpec((4, HEAD_DIM), const2),
            pl.BlockSpec((1, V_DIM), const2),
            pl.BlockSpec((d, d), const2),
            pl.BlockSpec((1, d), const2),
            pl.BlockSpec((1, d), const2),
        ],
        out_specs=pl.BlockSpec((1, tq, d), lambda bi, qi: (bi, qi, 0)),
        out_shape=jax.ShapeDtypeStruct((b, s, d), F32),
        scratch_shapes=[pltpu.VMEM((2 * tq, s), F32)] * 4,
        compiler_params=_cparams(2),
        name="diff_attn",
    )(q2, k, v, x, mod, lam, subln_g.reshape(1, V_DIM), wo_bf, ln_g.reshape(1, d), ln_b.reshape(1, d))


def _ffn_kernel(x_ref, mod_ref, wgu_ref, wd_ref, lng_ref, lnb_ref, o_ref, *, alpha):
    dff = wd_ref.shape[0]
    x = x_ref[0]
    h = (x * (1.0 + mod_ref[0, 4:5, :]) + mod_ref[0, 3:4, :]).astype(BF16)
    f = _swiglu(h, lambda c0, c1: wgu_ref[:, c0:c1], lambda c0, c1: wd_ref[c0:c1, :], dff)
    y = alpha * x + (1.0 + mod_ref[0, 5:6, :]) * f
    o_ref[0] = _ln(y, lng_ref[...], lnb_ref[...])


def _ffn_call(x, mod, wgu_bf, wd_bf, ln_g, ln_b, alpha):
    b, s, d = x.shape
    dff = wd_bf.shape[0]
    tm = min(512, s)
    const2 = lambda bi, si: (0, 0)
    return pl.pallas_call(
        functools.partial(_ffn_kernel, alpha=alpha),
        grid=(b, s // tm),
        in_specs=[
            pl.BlockSpec((1, tm, d), lambda bi, si: (bi, si, 0)),
            pl.BlockSpec((1, 6, d), lambda bi, si: (bi, 0, 0)),
            pl.BlockSpec((d, 2 * dff), const2, pipeline_mode=pl.Buffered(1)),
            pl.BlockSpec((dff, d), const2, pipeline_mode=pl.Buffered(1)),
            pl.BlockSpec((1, d), const2),
            pl.BlockSpec((1, d), const2),
        ],
        out_specs=pl.BlockSpec((1, tm, d), lambda bi, si: (bi, si, 0)),
        out_shape=jax.ShapeDtypeStruct((b, s, d), F32),
        compiler_params=_cparams(2),
        name="dense_swiglu",
    )(x, mod, wgu_bf, wd_bf, ln_g.reshape(1, d), ln_b.reshape(1, d))


def _sgu_kernel(x_ref, mod_ref, wuv_ref, buv_ref, lvg_ref, lvb_ref, ws_ref, bs_ref, wo_ref,
                lng_ref, lnb_ref, wr_ref, x2_ref, route_ref, *, alpha, n_sub):
    tm = x_ref.shape[1]
    sub = tm // n_sub
    width = wo_ref.shape[0]
    blocks = [(i * sub, (i + 1) * sub) for i in range(n_sub)]
    xs = [x_ref[0, r0:r1, :] for r0, r1 in blocks]
    hs = [(x * (1.0 + mod_ref[0, 1:2, :]) + mod_ref[0, 0:1, :]).astype(BF16) for x in xs]
    zs = [jnp.dot(h, wuv_ref[...], preferred_element_type=F32) + buv_ref[...] for h in hs]
    zs = [0.5 * z * (1.0 + lax.erf(z * (2.0 ** -0.5))) for z in zs]
    us = [z[:, :width] for z in zs]
    vs = [_ln(z[:, width:], lvg_ref[...], lvb_ref[...]).astype(BF16) for z in zs]

    def spatial(u, v):
        rows = []
        for n in range(sub // SGU_CHUNK):
            r0, r1 = n * SGU_CHUNK, (n + 1) * SGU_CHUNK
            cols = []
            for g in range(SGU_GROUPS):
                c0, c1 = g * LANES, (g + 1) * LANES
                sp = jnp.dot(ws_ref[g], v[r0:r1, c0:c1], preferred_element_type=F32) + bs_ref[g]
                cols.append(u[r0:r1, c0:c1] * sp)
            rows.append(jnp.concatenate(cols, axis=1))
        return jnp.concatenate(rows, axis=0).astype(BF16)

    gated = [spatial(u, v) for u, v in zip(us, vs)]
    mixes = [jnp.dot(g, wo_ref[...], preferred_element_type=F32) for g in gated]
    x2s = [_ln(alpha * x + (1.0 + mod_ref[0, 2:3, :]) * mix, lng_ref[...], lnb_ref[...])
           for x, mix in zip(xs, mixes)]
    for (r0, r1), x2 in zip(blocks, x2s):
        x2_ref[0, r0:r1, :] = x2

    def split_hi_lo(x2):
        h2 = x2 * (1.0 + mod_ref[0, 4:5, :]) + mod_ref[0, 3:4, :]
        h_hi = h2.astype(BF16)
        h_lo = (h2 - h_hi.astype(F32)).astype(BF16)
        return jnp.concatenate([h_hi, h_lo], axis=0)

    prods = [jnp.dot(split_hi_lo(x2), wr_ref[...], preferred_element_type=F32) for x2 in x2s]
    lane = lax.broadcasted_iota(jnp.int32, (sub, LANES), 1)
    neg = jnp.float32(-jnp.inf)
    for (r0, r1), p in zip(blocks, prods):
        both = p[:sub] + p[sub:]
        logits = both + pltpu.roll(both, LANES - N_EXPERTS, 1)
        logits = jnp.where(lane < N_EXPERTS, logits, neg)
        m1 = jnp.max(logits, axis=-1, keepdims=True)
        i1 = jnp.min(jnp.where(logits == m1, lane, LANES), axis=-1, keepdims=True)
        rest = jnp.where(lane == i1, neg, logits)
        m2 = jnp.max(rest, axis=-1, keepdims=True)
        i2 = jnp.min(jnp.where(rest == m2, lane, LANES), axis=-1, keepdims=True)
        t = jnp.exp(m2 - m1)
        w1 = 1.0 / (1.0 + t)
        w2 = t * w1
        route_ref[0, r0:r1, :] = jnp.where(
            lane == 0, i1.astype(F32),
            jnp.where(lane == 1, i2.astype(F32), jnp.where(lane == 2, w1, jnp.where(lane == 3, w2, 0.0))))


def _sgu_call(x, mod, wuv_bf, b_uv, ln_v_g, ln_v_b, ws_bf, b_s, wo_bf, ln_g, ln_b, w_router, alpha):
    b, s, d = x.shape
    width = wo_bf.shape[0]
    tm = min(512, s)
    n_sub = 2 if tm % (2 * SGU_CHUNK) == 0 else 1
    w_hi = w_router.astype(BF16)
    w_lo = (w_router - w_hi.astype(F32)).astype(BF16)
    wr_pad = (jnp.zeros((d, LANES), BF16).at[:, :N_EXPERTS].set(w_hi)
              .at[:, N_EXPERTS:2 * N_EXPERTS].set(w_lo))
    const2 = lambda bi, si: (0, 0)
    const3 = lambda bi, si: (0, 0, 0)
    return pl.pallas_call(
        functools.partial(_sgu_kernel, alpha=alpha, n_sub=n_sub),
        grid=(b, s // tm),
        in_specs=[
            pl.BlockSpec((1, tm, d), lambda bi, si: (bi, si, 0)),
            pl.BlockSpec((1, 6, d), lambda bi, si: (bi, 0, 0)),
            pl.BlockSpec((d, 2 * width), const2),
            pl.BlockSpec((1, 2 * width), const2),
            pl.BlockSpec((1, width), const2),
            pl.BlockSpec((1, width), const2),
            pl.BlockSpec((SGU_GROUPS, SGU_CHUNK, SGU_CHUNK), const3),
            pl.BlockSpec((SGU_GROUPS, SGU_CHUNK, 1), const3),
            pl.BlockSpec((width, d), const2),
            pl.BlockSpec((1, d), const2),
            pl.BlockSpec((1, d), const2),
            pl.BlockSpec((d, LANES), const2),
        ],
        out_specs=[
            pl.BlockSpec((1, tm, d), lambda bi, si: (bi, si, 0)),
            pl.BlockSpec((1, tm, LANES), lambda bi, si: (bi, si, 0)),
        ],
        out_shape=[
            jax.ShapeDtypeStruct((b, s, d), F32),
            jax.ShapeDtypeStruct((b, s, LANES), F32),
        ],
        compiler_params=_cparams(2),
        name="sgu_router",
    )(x, mod, wuv_bf, b_uv.reshape(1, -1), ln_v_g.reshape(1, -1), ln_v_b.reshape(1, -1), ws_bf,
      b_s.reshape(SGU_GROUPS, SGU_CHUNK, 1), wo_bf, ln_g.reshape(1, d), ln_b.reshape(1, d), wr_pad)


MOE_BLOCK = 256
MOE_TILE = 256
SEG_ALIGN = 8
SEG_BITS = 6
MOE_CAP = 640


def _segment_copies(info_ref, make_copy, action):
    for e in range(N_EXPERTS):
        local = info_ref[0, 0, e]
        n_units = lax.shift_right_logical(info_ref[0, 0, N_EXPERTS + e], 3)
        glob = info_ref[0, 0, 2 * N_EXPERTS + e]
        off = jnp.int32(0)
        for j in reversed(range(SEG_BITS)):
            rows = SEG_ALIGN << j
            bit = lax.shift_right_logical(n_units, j) & 1

            @pl.when(bit == 1)
            def _():
                action(make_copy(pl.multiple_of(local + off, SEG_ALIGN), pl.multiple_of(glob + off, SEG_ALIGN), rows))

            off = off + bit * rows


def _dispatch_kernel(info_ref, infop_ref, zfill_ref, lp_ref, x_ref, mod_ref, xs_ref, sbuf, sem):
    i = pl.program_id(0)
    n = pl.num_programs(0)
    tb, d = x_ref.shape
    slot = i % 2

    @pl.when(i == 0)
    def _():
        sbuf[1, 0:MOE_TILE, :] = jnp.zeros((MOE_TILE, d), F32)

        def zero_tile(t):
            start = pl.multiple_of(t * MOE_TILE, MOE_TILE)
            return pltpu.make_async_copy(sbuf.at[1, pl.ds(0, MOE_TILE)], xs_ref.at[pl.ds(start, MOE_TILE)], sem.at[2])

        for e in range(N_EXPERTS):
            @pl.when(zfill_ref[e] >= 0)
            def _():
                zero_tile(zfill_ref[e]).start()
        for e in range(N_EXPERTS):
            @pl.when(zfill_ref[e] >= 0)
            def _():
                zero_tile(zfill_ref[e]).wait()

        def spare(t, carry):
            cp = zero_tile(t)
            cp.start()
            cp.wait()
            return carry

        lax.fori_loop(zfill_ref[N_EXPERTS], zfill_ref[N_EXPERTS + 1], spare, 0)

    h2 = (x_ref[...] * (1.0 + mod_ref[0, 4:5, :]) + mod_ref[0, 3:4, :]).astype(BF16)
    row = lax.broadcasted_iota(jnp.int32, (MOE_CAP, tb), 0)
    perm = jnp.logical_or(row == lp_ref[0, 0:1, :], row == lp_ref[0, 1:2, :]).astype(BF16)
    sbuf[slot] = jnp.dot(perm, h2, preferred_element_type=F32)

    def copy_out(s):
        return lambda local, glob, rows: pltpu.make_async_copy(
            sbuf.at[s, pl.ds(local, rows)], xs_ref.at[pl.ds(glob, rows)], sem.at[s])

    _segment_copies(info_ref, copy_out(slot), lambda cp: cp.start())

    @pl.when(i >= 1)
    def _():
        _segment_copies(infop_ref, copy_out(1 - slot), lambda cp: cp.wait())

    @pl.when(i == n - 1)
    def _():
        _segment_copies(info_ref, copy_out(slot), lambda cp: cp.wait())


def _dispatch_call(x2_flat, mod, info, zfill, lp_rows, n_rows, blocks_per_batch):
    t, d = x2_flat.shape
    tb = MOE_BLOCK
    n_info = info.shape[-1]
    return pl.pallas_call(
        _dispatch_kernel,
        grid=(t // tb,),
        in_specs=[
            pl.BlockSpec((1, 1, n_info), lambda i: (i, 0, 0), memory_space=pltpu.SMEM),
            pl.BlockSpec((1, 1, n_info), lambda i: (jnp.maximum(i - 1, 0), 0, 0), memory_space=pltpu.SMEM),
            pl.BlockSpec(memory_space=pltpu.SMEM),
            pl.BlockSpec((1, TOP_K, tb), lambda i: (i, 0, 0)),
            pl.BlockSpec((tb, d), lambda i: (i, 0)),
            pl.BlockSpec((1, 6, d), lambda i: (i // blocks_per_batch, 0, 0)),
        ],
        out_specs=pl.BlockSpec(memory_space=pl.ANY),
        out_shape=jax.ShapeDtypeStruct((n_rows, d), F32),
        scratch_shapes=[pltpu.VMEM((2, MOE_CAP, d), F32), pltpu.SemaphoreType.DMA((3,))],
        compiler_params=_cparams(1),
        name="moe_dispatch",
    )(info, info, zfill, lp_rows, x2_flat, mod)


def _expert_kernel(te_ref, nv_ref, xs_ref, wgu_ref, wd_ref, ys_ref):
    del te_ref
    t = pl.program_id(0)
    dff = wd_ref.shape[1]

    @pl.when(t < nv_ref[0])
    def _():
        h = xs_ref[...].astype(BF16)
        ys_ref[...] = _swiglu(h, lambda c0, c1: wgu_ref[0, :, c0:c1], lambda c0, c1: wd_ref[0, c0:c1, :], dff)

    @pl.when(t >= nv_ref[0])
    def _():
        ys_ref[...] = jnp.zeros_like(ys_ref)


def _expert_call(tile_expert, n_valid, xs, wegu_bf, wed_bf, tm):
    n_rows, d = xs.shape
    dff = wed_bf.shape[1]
    grid_spec = pltpu.PrefetchScalarGridSpec(
        num_scalar_prefetch=2,
        grid=(n_rows // tm,),
        in_specs=[
            pl.BlockSpec((tm, d), lambda t, te, nv: (jnp.minimum(t, nv[0] - 1), 0)),
            pl.BlockSpec((1, d, 2 * dff), lambda t, te, nv: (te[t], 0, 0)),
            pl.BlockSpec((1, dff, d), lambda t, te, nv: (te[t], 0, 0)),
        ],
        out_specs=pl.BlockSpec((tm, d), lambda t, te, nv: (t, 0)),
    )
    return pl.pallas_call(
        _expert_kernel,
        grid_spec=grid_spec,
        out_shape=jax.ShapeDtypeStruct((n_rows, d), F32),
        compiler_params=_cparams(1),
        name="moe_experts",
    )(tile_expert, n_valid, xs, wegu_bf, wed_bf)


def _combine_kernel(info_ref, infon_ref, lp_ref, x_ref, mod_ref, route_ref, lng_ref, lnb_ref, ys_ref, o_ref,
                    ybuf, sem, *, alpha):
    i = pl.program_id(0)
    n = pl.num_programs(0)
    tb = x_ref.shape[0]
    slot = i % 2

    def copy_in(s):
        return lambda local, glob, rows: pltpu.make_async_copy(
            ys_ref.at[pl.ds(glob, rows)], ybuf.at[s, pl.ds(local, rows)], sem.at[s])

    @pl.when(i == 0)
    def _():
        ybuf[...] = jnp.zeros(ybuf.shape, F32)
        _segment_copies(info_ref, copy_in(0), lambda cp: cp.start())

    @pl.when(i + 1 < n)
    def _():
        _segment_copies(infon_ref, copy_in(1 - slot), lambda cp: cp.start())

    _segment_copies(info_ref, copy_in(slot), lambda cp: cp.wait())

    yb = ybuf[slot].astype(BF16)
    lane = lax.broadcasted_iota(jnp.int32, (tb, MOE_CAP), 1)
    pick = jnp.concatenate([lane == lp_ref[:, 0:1], lane == lp_ref[:, 1:2]], axis=0).astype(BF16)
    ysel = jnp.dot(pick, yb, preferred_element_type=F32)
    x = x_ref[...]
    f = route_ref[:, 2:3] * ysel[:tb] + route_ref[:, 3:4] * ysel[tb:]
    y = alpha * x + (1.0 + mod_ref[0, 5:6, :]) * f
    o_ref[...] = _ln(y, lng_ref[...], lnb_ref[...])


def _combine_call(x2_flat, mod, route_flat, info, lp_cols, ys, ln_g, ln_b, blocks_per_batch, alpha):
    t, d = x2_flat.shape
    tb = MOE_BLOCK
    n = t // tb
    n_info = info.shape[-1]
    return pl.pallas_call(
        functools.partial(_combine_kernel, alpha=alpha),
        grid=(n,),
        in_specs=[
            pl.BlockSpec((1, 1, n_info), lambda i: (i, 0, 0), memory_space=pltpu.SMEM),
            pl.BlockSpec((1, 1, n_info), lambda i: (jnp.minimum(i + 1, n - 1), 0, 0), memory_space=pltpu.SMEM),
            pl.BlockSpec((tb, TOP_K), lambda i: (i, 0)),
            pl.BlockSpec((tb, d), lambda i: (i, 0)),
            pl.BlockSpec((1, 6, d), lambda i: (i // blocks_per_batch, 0, 0)),
            pl.BlockSpec((tb, LANES), lambda i: (i, 0)),
            pl.BlockSpec((1, d), lambda i: (0, 0)),
            pl.BlockSpec((1, d), lambda i: (0, 0)),
            pl.BlockSpec(memory_space=pl.ANY),
        ],
        out_specs=pl.BlockSpec((tb, d), lambda i: (i, 0)),
        out_shape=jax.ShapeDtypeStruct((t, d), F32),
        scratch_shapes=[pltpu.VMEM((2, MOE_CAP, d), F32), pltpu.SemaphoreType.DMA((2,))],
        compiler_params=_cparams(1),
        name="moe_combine",
    )(info, info, lp_cols, x2_flat, mod, route_flat, ln_g.reshape(1, d), ln_b.reshape(1, d), ys)


def _moe(x2, mod, route, wegu_bf, wed_bf, ln_g, ln_b, alpha):
    b, s, d = x2.shape
    t = b * s
    tb, tm = MOE_BLOCK, MOE_TILE
    assert s % tb == 0 and TOP_K * tb + N_EXPERTS * (SEG_ALIGN - 1) <= MOE_CAP
    assert tb <= SEG_ALIGN << (SEG_BITS - 1) and SEG_ALIGN == 8
    nb = t // tb
    r = route.reshape(t, LANES)
    eb = r[:, :TOP_K].astype(jnp.int32).reshape(nb, tb * TOP_K)
    na = tb * TOP_K
    onehot = (eb[:, None, :] == jnp.arange(N_EXPERTS, dtype=jnp.int32)[None, :, None])
    tri = (jnp.arange(na)[:, None] <= jnp.arange(na)[None, :]).astype(BF16)
    csum = jnp.dot(onehot.astype(BF16).reshape(nb * N_EXPERTS, na), tri,
                   preferred_element_type=F32).astype(jnp.int32).reshape(nb, N_EXPERTS, na)
    seg_len = ((csum[:, :, -1] + SEG_ALIGN - 1) // SEG_ALIGN) * SEG_ALIGN
    seg_local = jnp.cumsum(seg_len, axis=1) - seg_len
    lp = jnp.sum(jnp.where(onehot, csum - 1 + seg_local[:, :, None], 0), axis=1)
    rows_e = jnp.sum(seg_len, axis=0)
    padded = ((rows_e + tm - 1) // tm) * tm
    ends = jnp.cumsum(padded)
    starts = ends - padded
    seg_glob = starts[None, :] + jnp.cumsum(seg_len, axis=0) - seg_len
    info = jnp.concatenate([seg_local, seg_len, seg_glob], axis=1).astype(jnp.int32).reshape(nb, 1, 3 * N_EXPERTS)
    lp_rows = lp.reshape(nb, tb, TOP_K).transpose(0, 2, 1).astype(jnp.int32)
    lp_cols = lp.reshape(t, TOP_K).astype(jnp.int32)
    n_tiles = (t * TOP_K + nb * N_EXPERTS * (SEG_ALIGN - 1) + tm - 1) // tm + N_EXPERTS
    tile_start = jnp.arange(n_tiles, dtype=jnp.int32) * tm
    tile_expert = jnp.minimum(jnp.sum((tile_start[:, None] >= ends[None, :]).astype(jnp.int32), axis=1),
                              N_EXPERTS - 1).astype(jnp.int32)
    n_valid = (ends[-1:] // tm).astype(jnp.int32)
    zfill = jnp.concatenate([jnp.where(padded > 0, ends // tm - 1, -1), n_valid,
                             jnp.full((1,), n_tiles)]).astype(jnp.int32)
    x2_flat = x2.reshape(t, d)
    xs = _dispatch_call(x2_flat, mod, info, zfill, lp_rows, n_tiles * tm, s // tb)
    ys = _expert_call(tile_expert, n_valid, xs, wegu_bf, wed_bf, tm)
    out = _combine_call(x2_flat, mod, r, info, lp_cols, ys, ln_g, ln_b, s // tb, alpha)
    return out.reshape(b, s, d)


def kernel(x_prompt, x_sample, c_prompt, c_sample, w_ada, b_ada, ln_g, ln_b, w_qkv, diff_lambda,
           subln_g, w_o_attn, w_uv, b_uv, ln_v_g, ln_v_b, w_s, b_s, w_o_sgu, w_ffn_gu, w_ffn_down,
           w_router, w_e_gu, w_e_down):
    depth = w_ada.shape[0]
    d = x_prompt.shape[-1]
    alpha = (2 * depth) ** 0.25
    bp = c_prompt.shape[0]
    mods = _ada_call(jnp.concatenate([c_prompt, c_sample], axis=0), w_ada, b_ada)

    w_qkv_bf = w_qkv.astype(BF16)
    w_o_attn_bf = w_o_attn.astype(BF16)
    w_ffn_gu_bf = w_ffn_gu.astype(BF16)
    w_ffn_down_bf = w_ffn_down.astype(BF16)
    w_uv_bf = w_uv.astype(BF16)
    w_s_bf = w_s.astype(BF16)
    w_o_sgu_bf = w_o_sgu.astype(BF16)
    w_e_gu_bf = w_e_gu.astype(BF16)
    w_e_down_bf = w_e_down.astype(BF16)

    def trunk(x, r0, r1):
        for i in range(depth):
            j = i // 2
            mod = mods[i, r0:r1].reshape(r1 - r0, 6, d)
            if i % 2 == 0:
                lambda_init = 0.8 - 0.6 * math.exp(-0.3 * i)
                q2, k, v = _qkv_call(x, mod, w_qkv_bf[j])
                x = _attn_call(q2, k, v, x, mod, diff_lambda[j], subln_g[j], w_o_attn_bf[j],
                               ln_g[i, 0], ln_b[i, 0], lambda_init, alpha)
                x = _ffn_call(x, mod, w_ffn_gu_bf[j], w_ffn_down_bf[j], ln_g[i, 1], ln_b[i, 1], alpha)
            else:
                x, route = _sgu_call(x, mod, w_uv_bf[j], b_uv[j], ln_v_g[j], ln_v_b[j], w_s_bf[j], b_s[j],
                                     w_o_sgu_bf[j], ln_g[i, 0], ln_b[i, 0], w_router[j], alpha)
                x = _moe(x, mod, route, w_e_gu_bf[j], w_e_down_bf[j], ln_g[i, 1], ln_b[i, 1], alpha)
        return x

    y_prompt = trunk(x_prompt, 0, bp)
    y_sample = trunk(x_sample, bp, bp + c_sample.shape[0])
    return (y_prompt, y_sample)
```

```python
import functools
import math

import jax
import jax.numpy as jnp
from jax import lax
from jax.experimental import pallas as pl
from jax.experimental.pallas import tpu as pltpu

F32 = jnp.float32
BF16 = jnp.bfloat16

LANES = 128
N_HEADS = 8
HEAD_DIM = 64
V_DIM = 2 * HEAD_DIM
ROPE_THETA = 10000.0
SGU_CHUNK = 128
SGU_GROUPS = 8
N_EXPERTS = 8
TOP_K = 2
LN_EPS = 1e-5
VMEM_LIMIT_BYTES = 56 * 1024 * 1024

HIGHEST = lax.Precision.HIGHEST
LOG2_E = 1.4426950408889634
ATTN_KEY_CHUNK = 256


def _cparams(n_axes):
    return pltpu.CompilerParams(
        dimension_semantics=("arbitrary",) * n_axes,
        vmem_limit_bytes=VMEM_LIMIT_BYTES,
    )


def _ln(y, g, b):
    mu = jnp.mean(y, axis=-1, keepdims=True)
    yc = y - mu
    var = jnp.mean(yc * yc, axis=-1, keepdims=True)
    return yc * lax.rsqrt(var + LN_EPS) * g + b


def _ff_chunks(dff, chunk=512):
    out, c0 = [], 0
    while c0 < dff:
        c1 = min(c0 + chunk, dff)
        out.append((c0, c1))
        c0 = c1
    return out


def _swiglu(h, wgu_at, wd_at, dff):
    f = None
    for c0, c1 in _ff_chunks(dff):
        g = jnp.dot(h, wgu_at(c0, c1), preferred_element_type=F32)
        u = jnp.dot(h, wgu_at(dff + c0, dff + c1), preferred_element_type=F32)
        a = (g * jax.nn.sigmoid(g) * u).astype(BF16)
        p = jnp.dot(a, wd_at(c0, c1), preferred_element_type=F32)
        f = p if f is None else f + p
    return f


def _ada_kernel(c_ref, w_ref, b_ref, o_ref):
    c = c_ref[...]
    s = c * jax.nn.sigmoid(c)
    o_ref[0] = jnp.dot(s, w_ref[0], preferred_element_type=F32, precision=HIGHEST) + b_ref[0]


def _ada_call(c_all, w_ada, b_ada):
    depth, d, n6 = w_ada.shape
    nb = c_all.shape[0]
    tn = 1536
    return pl.pallas_call(
        _ada_kernel,
        grid=(depth, n6 // tn),
        in_specs=[
            pl.BlockSpec((nb, d), lambda l, j: (0, 0)),
            pl.BlockSpec((1, d, tn), lambda l, j: (l, 0, j)),
            pl.BlockSpec((1, 1, tn), lambda l, j: (l, 0, j)),
        ],
        out_specs=pl.BlockSpec((1, nb, tn), lambda l, j: (l, 0, j)),
        out_shape=jax.ShapeDtypeStruct((depth, nb, n6), F32),
        compiler_params=_cparams(2),
        name="ada",
    )(c_all, w_ada, b_ada.reshape(depth, 1, n6))


def _qkv_kernel(x_ref, mod_ref, w_ref, cq_ref, sq_ref, ck_ref, sk_ref, q_ref, k_ref, v_ref):
    d = x_ref.shape[-1]
    ts = x_ref.shape[1]
    x = x_ref[0]
    h = (x * (1.0 + mod_ref[0, 1:2, :]) + mod_ref[0, 0:1, :]).astype(BF16)
    qkv = jnp.dot(h, w_ref[...], preferred_element_type=F32)
    lane = lax.broadcasted_iota(jnp.int32, (ts, LANES), 1)
    first_half = (lane & (HEAD_DIM // 2)) == 0
    is_map1 = lane < HEAD_DIM
    cq, sq, ck, sk = cq_ref[...], sq_ref[...], ck_ref[...], sk_ref[...]

    def rope(t, cos, sin_signed):
        r = jnp.where(first_half, pltpu.roll(t, LANES - HEAD_DIM // 2, 1), pltpu.roll(t, HEAD_DIM // 2, 1))
        return t * cos + r * sin_signed

    for j in range(N_HEADS):
        qr = rope(qkv[:, j * LANES:(j + 1) * LANES], cq, sq)
        q_ref[0, 0, j] = jnp.where(is_map1, qr, 0.0).astype(BF16)
        q_ref[0, 1, j] = jnp.where(is_map1, 0.0, qr).astype(BF16)
        k_ref[0, j] = rope(qkv[:, d + j * LANES:d + (j + 1) * LANES], ck, sk).astype(BF16)
        v_ref[0, j] = qkv[:, 2 * d + j * LANES:2 * d + (j + 1) * LANES].astype(BF16)


def _rope_tables(seq):
    half = HEAD_DIM // 2
    inv = 1.0 / (ROPE_THETA ** (jnp.arange(0, HEAD_DIM, 2, dtype=F32) / HEAD_DIM))
    ang = jnp.arange(seq, dtype=F32)[:, None] * inv[None, :]
    cos = jnp.tile(jnp.cos(ang), (1, 2 * LANES // HEAD_DIM))
    sin = jnp.sin(ang)
    sin_signed = jnp.tile(jnp.concatenate([-sin, sin], axis=-1), (1, LANES // HEAD_DIM))
    del half
    scale = HEAD_DIM ** -0.5 * LOG2_E
    return cos * scale, sin_signed * scale, cos, sin_signed


def _qkv_call(x, mod, w_qkv_bf):
    b, s, d = x.shape
    ts = min(512, s)
    cq, sq, ck, sk = _rope_tables(s)
    tab = pl.BlockSpec((ts, LANES), lambda bi, si: (si, 0))
    return pl.pallas_call(
        _qkv_kernel,
        grid=(b, s // ts),
        in_specs=[
            pl.BlockSpec((1, ts, d), lambda bi, si: (bi, si, 0)),
            pl.BlockSpec((1, 6, d), lambda bi, si: (bi, 0, 0)),
            pl.BlockSpec((d, 3 * d), lambda bi, si: (0, 0)),
            tab, tab, tab, tab,
        ],
        out_specs=[
            pl.BlockSpec((1, 2, N_HEADS, ts, LANES), lambda bi, si: (bi, 0, 0, si, 0)),
            pl.BlockSpec((1, N_HEADS, ts, LANES), lambda bi, si: (bi, 0, si, 0)),
            pl.BlockSpec((1, N_HEADS, ts, LANES), lambda bi, si: (bi, 0, si, 0)),
        ],
        out_shape=[
            jax.ShapeDtypeStruct((b, 2, N_HEADS, s, LANES), BF16),
            jax.ShapeDtypeStruct((b, N_HEADS, s, LANES), BF16),
            jax.ShapeDtypeStruct((b, N_HEADS, s, LANES), BF16),
        ],
        compiler_params=_cparams(2),
        name="qkv_rope",
    )(x, mod, w_qkv_bf, cq, sq, ck, sk)


def _attn_kernel(q_ref, k_ref, v_ref, x_ref, mod_ref, lam_ref, sg_ref, wo_ref, lng_ref, lnb_ref,
                 o_ref, s_buf0, s_buf1, e_buf0, e_buf1, *, lambda_init, alpha):
    s_buf = (s_buf0, s_buf1)
    e_buf = (e_buf0, e_buf1)
    tq = x_ref.shape[1]
    ck = ATTN_KEY_CHUNK
    n_chunks = k_ref.shape[2] // ck
    lam = lam_ref[...]
    lam_full = (jnp.exp(jnp.sum(lam[0:1] * lam[1:2], keepdims=True))
                - jnp.exp(jnp.sum(lam[2:3] * lam[3:4], keepdims=True)) + lambda_init)

    def fold_lanes(t, op):
        out = t[:, :LANES]
        for c in range(1, ck // LANES):
            out = op(out, t[:, c * LANES:(c + 1) * LANES])
        return out

    qs, mpart, lpart, acc, m_row, r_row, inv_row = {}, {}, {}, {}, {}, {}, {}
    heads = []
    for i in range(-1, N_HEADS + 1):
        ha, hb, hc = i + 1, i, i - 1
        if 0 <= hb < N_HEADS:
            m_row[hb] = jnp.max(mpart.pop(hb), axis=-1, keepdims=True)
        if 0 <= hc < N_HEADS:
            l = jnp.sum(lpart.pop(hc), axis=-1, keepdims=True)
            r_row[hc] = lam_full * l[:tq] / l[tq:]
            inv_row[hc] = 1.0 / l[:tq]
        if ha < N_HEADS:
            qs[ha] = q_ref[0, :, ha].reshape(2 * tq, LANES)
        for j in range(n_chunks):
            cols = slice(j * ck, (j + 1) * ck)
            if ha < N_HEADS:
                s = lax.dot_general(qs[ha], k_ref[0, ha, cols, :], (((1,), (1,)), ((), ())),
                                    preferred_element_type=F32)
                s_buf[ha % 2][:, cols] = s
                sm = fold_lanes(s, jnp.maximum)
                mpart[ha] = sm if j == 0 else jnp.maximum(mpart[ha], sm)
            if 0 <= hb < N_HEADS:
                e = jnp.exp2(s_buf[hb % 2][:, cols] - m_row[hb])
                e_buf[hb % 2][:, cols] = e
                es = fold_lanes(e, jnp.add)
                lpart[hb] = es if j == 0 else lpart[hb] + es
            if 0 <= hc < N_HEADS:
                a = (e_buf[hc % 2][:tq, cols] - e_buf[hc % 2][tq:, cols] * r_row[hc]).astype(BF16)
                p = jnp.dot(a, v_ref[0, hc, cols, :], preferred_element_type=F32)
                acc[hc] = p if j == 0 else acc[hc] + p
        if 0 <= hc < N_HEADS:
            oh = acc.pop(hc) * inv_row[hc]
            ms = jnp.mean(oh * oh, axis=-1, keepdims=True)
            oh = oh * lax.rsqrt(ms + LN_EPS) * sg_ref[...] * (1.0 - lambda_init)
            heads.append(oh.astype(BF16))
    o = jnp.concatenate(heads, axis=1)
    mix = jnp.dot(o, wo_ref[...], preferred_element_type=F32)
    y = alpha * x_ref[0] + (1.0 + mod_ref[0, 2:3, :]) * mix
    o_ref[0] = _ln(y, lng_ref[...], lnb_ref[...])


def _attn_call(q2, k, v, x, mod, lam, subln_g, wo_bf, ln_g, ln_b, lambda_init, alpha):
    b, s, d = x.shape
    tq = min(256, s)
    const2 = lambda bi, qi: (0, 0)
    return pl.pallas_call(
        functools.partial(_attn_kernel, lambda_init=lambda_init, alpha=alpha),
        grid=(b, s // tq),
        in_specs=[
            pl.BlockSpec((1, 2, N_HEADS, tq, LANES), lambda bi, qi: (bi, 0, 0, qi, 0)),
            pl.BlockSpec((1, N_HEADS, s, LANES), lambda bi, qi: (bi, 0, 0, 0)),
            pl.BlockSpec((1, N_HEADS, s, LANES), lambda bi, qi: (bi, 0, 0, 0)),
            pl.BlockSpec((1, tq, d), lambda bi, qi: (bi, qi, 0)),
            pl.BlockSpec((1, 6, d), lambda bi, qi: (bi, 0, 0)),
            pl.BlockSpec((4, HEAD_DIM), const2),
            pl.BlockSpec((1, V_DIM), const2),
            pl.BlockSpec((d, d), const2),
            pl.BlockSpec((1, d), const2),
            pl.BlockSpec((1, d), const2),
        ],
        out_specs=pl.BlockSpec((1, tq, d), lambda bi, qi: (bi, qi, 0)),
        out_shape=jax.ShapeDtypeStruct((b, s, d), F32),
        scratch_shapes=[pltpu.VMEM((2 * tq, s), F32)] * 4,
        compiler_params=_cparams(2),
        name="diff_attn",
    )(q2, k, v, x, mod, lam, subln_g.reshape(1, V_DIM), wo_bf, ln_g.reshape(1, d), ln_b.reshape(1, d))


def _ffn_kernel(x_ref, mod_ref, wgu_ref, wd_ref, lng_ref, lnb_ref, o_ref, *, alpha):
    dff = wd_ref.shape[0]
    x = x_ref[0]
    h = (x * (1.0 + mod_ref[0, 4:5, :]) + mod_ref[0, 3:4, :]).astype(BF16)
    f = _swiglu(h, lambda c0, c1: wgu_ref[:, c0:c1], lambda c0, c1: wd_ref[c0:c1, :], dff)
    y = alpha * x + (1.0 + mod_ref[0, 5:6, :]) * f
    o_ref[0] = _ln(y, lng_ref[...], lnb_ref[...])


def _ffn_call(x, mod, wgu_bf, wd_bf, ln_g, ln_b, alpha):
    b, s, d = x.shape
    dff = wd_bf.shape[0]
    tm = min(512, s)
    const2 = lambda bi, si: (0, 0)
    return pl.pallas_call(
        functools.partial(_ffn_kernel, alpha=alpha),
        grid=(b, s // tm),
        in_specs=[
            pl.BlockSpec((1, tm, d), lambda bi, si: (bi, si, 0)),
            pl.BlockSpec((1, 6, d), lambda bi, si: (bi, 0, 0)),
            pl.BlockSpec((d, 2 * dff), const2, pipeline_mode=pl.Buffered(1)),
            pl.BlockSpec((dff, d), const2, pipeline_mode=pl.Buffered(1)),
            pl.BlockSpec((1, d), const2),
            pl.BlockSpec((1, d), const2),
        ],
        out_specs=pl.BlockSpec((1, tm, d), lambda bi, si: (bi, si, 0)),
        out_shape=jax.ShapeDtypeStruct((b, s, d), F32),
        compiler_params=_cparams(2),
        name="dense_swiglu",
    )(x, mod, wgu_bf, wd_bf, ln_g.reshape(1, d), ln_b.reshape(1, d))


def _sgu_kernel(x_ref, mod_ref, wuv_ref, buv_ref, lvg_ref, lvb_ref, ws_ref, bs_ref, wo_ref,
                lng_ref, lnb_ref, wr_ref, x2_ref, route_ref, route_t_ref, *, alpha, n_sub):
    tm = x_ref.shape[1]
    sub = tm // n_sub
    width = wo_ref.shape[0]
    blocks = [(i * sub, (i + 1) * sub) for i in range(n_sub)]
    xs = [x_ref[0, r0:r1, :] for r0, r1 in blocks]
    hs = [(x * (1.0 + mod_ref[0, 1:2, :]) + mod_ref[0, 0:1, :]).astype(BF16) for x in xs]
    zs = [jnp.dot(h, wuv_ref[...], preferred_element_type=F32) + buv_ref[...] for h in hs]
    zs = [0.5 * z * (1.0 + lax.erf(z * (2.0 ** -0.5))) for z in zs]
    us = [z[:, :width] for z in zs]
    vs = [_ln(z[:, width:], lvg_ref[...], lvb_ref[...]).astype(BF16) for z in zs]

    def spatial(u, v):
        rows = []
        for n in range(sub // SGU_CHUNK):
            r0, r1 = n * SGU_CHUNK, (n + 1) * SGU_CHUNK
            cols = []
            for g in range(SGU_GROUPS):
                c0, c1 = g * LANES, (g + 1) * LANES
                sp = jnp.dot(ws_ref[g], v[r0:r1, c0:c1], preferred_element_type=F32) + bs_ref[g]
                cols.append(u[r0:r1, c0:c1] * sp)
            rows.append(jnp.concatenate(cols, axis=1))
        return jnp.concatenate(rows, axis=0).astype(BF16)

    gated = [spatial(u, v) for u, v in zip(us, vs)]
    mixes = [jnp.dot(g, wo_ref[...], preferred_element_type=F32) for g in gated]
    x2s = [_ln(alpha * x + (1.0 + mod_ref[0, 2:3, :]) * mix, lng_ref[...], lnb_ref[...])
           for x, mix in zip(xs, mixes)]
    for (r0, r1), x2 in zip(blocks, x2s):
        x2_ref[0, r0:r1, :] = x2

    def split_hi_lo(x2):
        h2 = x2 * (1.0 + mod_ref[0, 4:5, :]) + mod_ref[0, 3:4, :]
        h_hi = h2.astype(BF16)
        h_lo = (h2 - h_hi.astype(F32)).astype(BF16)
        return jnp.concatenate([h_hi, h_lo], axis=0)

    prods = [jnp.dot(split_hi_lo(x2), wr_ref[...], preferred_element_type=F32) for x2 in x2s]
    eidx = lax.broadcasted_iota(jnp.int32, (N_EXPERTS, sub), 0)
    neg = jnp.float32(-jnp.inf)
    for blk, ((r0, r1), p) in enumerate(zip(blocks, prods)):
        both = p[:sub] + p[sub:]
        logits = both + pltpu.roll(both, LANES - N_EXPERTS, 1)
        lt = jnp.transpose(logits)[0:N_EXPERTS, :]
        m1 = jnp.max(lt, axis=0, keepdims=True)
        i1 = jnp.min(jnp.where(lt == m1, eidx, N_EXPERTS), axis=0, keepdims=True)
        rest = jnp.where(eidx == i1, neg, lt)
        m2 = jnp.max(rest, axis=0, keepdims=True)
        i2 = jnp.min(jnp.where(rest == m2, eidx, N_EXPERTS), axis=0, keepdims=True)
        t = jnp.exp(m2 - m1)
        w1 = 1.0 / (1.0 + t)
        w2 = t * w1
        rt = jnp.where(eidx == 0, i1.astype(F32),
                       jnp.where(eidx == 1, i2.astype(F32), jnp.where(eidx == 2, w1, jnp.where(eidx == 3, w2, 0.0))))
        route_t_ref[0, blk] = rt
        padded = jnp.concatenate([rt, jnp.zeros((LANES - N_EXPERTS, sub), F32)], axis=0)
        route_ref[0, r0:r1, :] = jnp.transpose(padded)


def _sgu_call(x, mod, wuv_bf, b_uv, ln_v_g, ln_v_b, ws_bf, b_s, wo_bf, ln_g, ln_b, w_router, alpha):
    b, s, d = x.shape
    width = wo_bf.shape[0]
    tm = min(512, s)
    assert tm % MOE_BLOCK == 0, "each row block's routing rows feed one dispatch / combine block"
    n_sub = tm // MOE_BLOCK
    w_hi = w_router.astype(BF16)
    w_lo = (w_router - w_hi.astype(F32)).astype(BF16)
    wr_pad = (jnp.zeros((d, LANES), BF16).at[:, :N_EXPERTS].set(w_hi)
              .at[:, N_EXPERTS:2 * N_EXPERTS].set(w_lo))
    const2 = lambda bi, si: (0, 0)
    const3 = lambda bi, si: (0, 0, 0)
    return pl.pallas_call(
        functools.partial(_sgu_kernel, alpha=alpha, n_sub=n_sub),
        grid=(b, s // tm),
        in_specs=[
            pl.BlockSpec((1, tm, d), lambda bi, si: (bi, si, 0)),
            pl.BlockSpec((1, 6, d), lambda bi, si: (bi, 0, 0)),
            pl.BlockSpec((d, 2 * width), const2),
            pl.BlockSpec((1, 2 * width), const2),
            pl.BlockSpec((1, width), const2),
            pl.BlockSpec((1, width), const2),
            pl.BlockSpec((SGU_GROUPS, SGU_CHUNK, SGU_CHUNK), const3),
            pl.BlockSpec((SGU_GROUPS, SGU_CHUNK, 1), const3),
            pl.BlockSpec((width, d), const2),
            pl.BlockSpec((1, d), const2),
            pl.BlockSpec((1, d), const2),
            pl.BlockSpec((d, LANES), const2),
        ],
        out_specs=[
            pl.BlockSpec((1, tm, d), lambda bi, si: (bi, si, 0)),
            pl.BlockSpec((1, tm, LANES), lambda bi, si: (bi, si, 0)),
            pl.BlockSpec((1, n_sub, N_EXPERTS, MOE_BLOCK), lambda bi, si: (bi, si, 0, 0)),
        ],
        out_shape=[
            jax.ShapeDtypeStruct((b, s, d), F32),
            jax.ShapeDtypeStruct((b, s, LANES), F32),
            jax.ShapeDtypeStruct((b, s // MOE_BLOCK, N_EXPERTS, MOE_BLOCK), F32),
        ],
        compiler_params=_cparams(2),
        name="sgu_router",
    )(x, mod, wuv_bf, b_uv.reshape(1, -1), ln_v_g.reshape(1, -1), ln_v_b.reshape(1, -1), ws_bf,
      b_s.reshape(SGU_GROUPS, SGU_CHUNK, 1), wo_bf, ln_g.reshape(1, d), ln_b.reshape(1, d), wr_pad)


MOE_BLOCK = 256
MOE_TILE = 256
SEG_ALIGN = 8
SEG_BITS = 6
MOE_CAP = 640


def _segment_copies(info_ref, make_copy, action):
    for e in range(N_EXPERTS):
        local = info_ref[0, 0, e]
        n_units = lax.shift_right_logical(info_ref[0, 0, N_EXPERTS + e], 3)
        glob = info_ref[0, 0, 2 * N_EXPERTS + e]
        off = jnp.int32(0)
        for j in reversed(range(SEG_BITS)):
            rows = SEG_ALIGN << j
            bit = lax.shift_right_logical(n_units, j) & 1

            @pl.when(bit == 1)
            def _():
                action(make_copy(pl.multiple_of(local + off, SEG_ALIGN), pl.multiple_of(glob + off, SEG_ALIGN), rows))

            off = off + bit * rows


def _dispatch_kernel(info_ref, infop1_ref, infop2_ref, zfill_ref, lp_ref, x_ref, mod_ref, xs_ref, sbuf, sem):
    i = pl.program_id(0)
    n = pl.num_programs(0)
    tb, d = x_ref.shape
    slot = i % DISPATCH_SLOTS

    @pl.when(i == 0)
    def _():
        sbuf[1, 0:MOE_TILE, :] = jnp.zeros((MOE_TILE, d), F32)

        def zero_tile(t):
            start = pl.multiple_of(t * MOE_TILE, MOE_TILE)
            return pltpu.make_async_copy(sbuf.at[1, pl.ds(0, MOE_TILE)], xs_ref.at[pl.ds(start, MOE_TILE)],
                                         sem.at[DISPATCH_SLOTS])

        for e in range(N_EXPERTS):
            @pl.when(zfill_ref[e] >= 0)
            def _():
                zero_tile(zfill_ref[e]).start()
        for e in range(N_EXPERTS):
            @pl.when(zfill_ref[e] >= 0)
            def _():
                zero_tile(zfill_ref[e]).wait()

        def spare(t, carry):
            cp = zero_tile(t)
            cp.start()
            cp.wait()
            return carry

        lax.fori_loop(zfill_ref[N_EXPERTS], zfill_ref[N_EXPERTS + 1], spare, 0)

    h2 = (x_ref[...] * (1.0 + mod_ref[0, 4:5, :]) + mod_ref[0, 3:4, :]).astype(BF16)
    row = lax.broadcasted_iota(jnp.int32, (MOE_CAP, tb), 0)
    perm = jnp.logical_or(row == lp_ref[0, 0:1, :], row == lp_ref[0, 1:2, :]).astype(BF16)
    sbuf[slot] = jnp.dot(perm, h2, preferred_element_type=F32)

    def copy_out(s):
        return lambda local, glob, rows: pltpu.make_async_copy(
            sbuf.at[s, pl.ds(local, rows)], xs_ref.at[pl.ds(glob, rows)], sem.at[s])

    _segment_copies(info_ref, copy_out(slot), lambda cp: cp.start())

    @pl.when(i >= 2)
    def _():
        _segment_copies(infop2_ref, copy_out((i + 1) % DISPATCH_SLOTS), lambda cp: cp.wait())

    @pl.when(jnp.logical_and(i == n - 1, i >= 1))
    def _():
        _segment_copies(infop1_ref, copy_out((i + 2) % DISPATCH_SLOTS), lambda cp: cp.wait())

    @pl.when(i == n - 1)
    def _():
        _segment_copies(info_ref, copy_out(slot), lambda cp: cp.wait())


DISPATCH_SLOTS = 3


def _dispatch_call(x2_flat, mod, info, zfill, lp_rows, n_rows, blocks_per_batch):
    t, d = x2_flat.shape
    tb = MOE_BLOCK
    n_info = info.shape[-1]
    return pl.pallas_call(
        _dispatch_kernel,
        grid=(t // tb,),
        in_specs=[
            pl.BlockSpec((1, 1, n_info), lambda i: (i, 0, 0), memory_space=pltpu.SMEM),
            pl.BlockSpec((1, 1, n_info), lambda i: (jnp.maximum(i - 1, 0), 0, 0), memory_space=pltpu.SMEM),
            pl.BlockSpec((1, 1, n_info), lambda i: (jnp.maximum(i - 2, 0), 0, 0), memory_space=pltpu.SMEM),
            pl.BlockSpec(memory_space=pltpu.SMEM),
            pl.BlockSpec((1, TOP_K, tb), lambda i: (i, 0, 0)),
            pl.BlockSpec((tb, d), lambda i: (i, 0)),
            pl.BlockSpec((1, 6, d), lambda i: (i // blocks_per_batch, 0, 0)),
        ],
        out_specs=pl.BlockSpec(memory_space=pl.ANY),
        out_shape=jax.ShapeDtypeStruct((n_rows, d), F32),
        scratch_shapes=[pltpu.VMEM((DISPATCH_SLOTS, MOE_CAP, d), F32),
                        pltpu.SemaphoreType.DMA((DISPATCH_SLOTS + 1,))],
        compiler_params=_cparams(1),
        name="moe_dispatch",
    )(info, info, info, zfill, lp_rows, x2_flat, mod)


def _expert_kernel(te_ref, nv_ref, xs_ref, wgu_ref, wd_ref, ys_ref):
    del te_ref
    t = pl.program_id(0)
    dff = wd_ref.shape[1]

    @pl.when(t < nv_ref[0])
    def _():
        h = xs_ref[...].astype(BF16)
        ys_ref[...] = _swiglu(h, lambda c0, c1: wgu_ref[0, :, c0:c1], lambda c0, c1: wd_ref[0, c0:c1, :], dff)

    @pl.when(t >= nv_ref[0])
    def _():
        ys_ref[...] = jnp.zeros_like(ys_ref)


def _expert_call(tile_expert, n_valid, xs, wegu_bf, wed_bf, tm):
    n_rows, d = xs.shape
    dff = wed_bf.shape[1]
    grid_spec = pltpu.PrefetchScalarGridSpec(
        num_scalar_prefetch=2,
        grid=(n_rows // tm,),
        in_specs=[
            pl.BlockSpec((tm, d), lambda t, te, nv: (jnp.minimum(t, nv[0] - 1), 0)),
            pl.BlockSpec((1, d, 2 * dff), lambda t, te, nv: (te[t], 0, 0)),
            pl.BlockSpec((1, dff, d), lambda t, te, nv: (te[t], 0, 0)),
        ],
        out_specs=pl.BlockSpec((tm, d), lambda t, te, nv: (t, 0)),
    )
    return pl.pallas_call(
        _expert_kernel,
        grid_spec=grid_spec,
        out_shape=jax.ShapeDtypeStruct((n_rows, d), F32),
        compiler_params=_cparams(1),
        name="moe_experts",
    )(tile_expert, n_valid, xs, wegu_bf, wed_bf)


def _combine_kernel(info_ref, infon_ref, lp_ref, x_ref, mod_ref, route_ref, lng_ref, lnb_ref, ys_ref, o_ref,
                    ybuf, sem, *, alpha):
    i = pl.program_id(0)
    n = pl.num_programs(0)
    tb = x_ref.shape[0]
    slot = i % 2

    def copy_in(s):
        return lambda local, glob, rows: pltpu.make_async_copy(
            ys_ref.at[pl.ds(glob, rows)], ybuf.at[s, pl.ds(local, rows)], sem.at[s])

    @pl.when(i == 0)
    def _():
        ybuf[...] = jnp.zeros(ybuf.shape, F32)
        _segment_copies(info_ref, copy_in(0), lambda cp: cp.start())

    @pl.when(i + 1 < n)
    def _():
        _segment_copies(infon_ref, copy_in(1 - slot), lambda cp: cp.start())

    _segment_copies(info_ref, copy_in(slot), lambda cp: cp.wait())

    yb = ybuf[slot].astype(BF16)
    lane = lax.broadcasted_iota(jnp.int32, (tb, MOE_CAP), 1)
    pick = jnp.concatenate([lane == lp_ref[:, 0:1], lane == lp_ref[:, 1:2]], axis=0).astype(BF16)
    ysel = jnp.dot(pick, yb, preferred_element_type=F32)
    x = x_ref[...]
    f = route_ref[:, 2:3] * ysel[:tb] + route_ref[:, 3:4] * ysel[tb:]
    y = alpha * x + (1.0 + mod_ref[0, 5:6, :]) * f
    o_ref[...] = _ln(y, lng_ref[...], lnb_ref[...])


def _combine_call(x2_flat, mod, route_flat, info, lp_cols, ys, ln_g, ln_b, blocks_per_batch, alpha):
    t, d = x2_flat.shape
    tb = MOE_BLOCK
    n = t // tb
    n_info = info.shape[-1]
    return pl.pallas_call(
        functools.partial(_combine_kernel, alpha=alpha),
        grid=(n,),
        in_specs=[
            pl.BlockSpec((1, 1, n_info), lambda i: (i, 0, 0), memory_space=pltpu.SMEM),
            pl.BlockSpec((1, 1, n_info), lambda i: (jnp.minimum(i + 1, n - 1), 0, 0), memory_space=pltpu.SMEM),
            pl.BlockSpec((tb, TOP_K), lambda i: (i, 0)),
            pl.BlockSpec((tb, d), lambda i: (i, 0)),
            pl.BlockSpec((1, 6, d), lambda i: (i // blocks_per_batch, 0, 0)),
            pl.BlockSpec((tb, LANES), lambda i: (i, 0)),
            pl.BlockSpec((1, d), lambda i: (0, 0)),
            pl.BlockSpec((1, d), lambda i: (0, 0)),
            pl.BlockSpec(memory_space=pl.ANY),
        ],
        out_specs=pl.BlockSpec((tb, d), lambda i: (i, 0)),
        out_shape=jax.ShapeDtypeStruct((t, d), F32),
        scratch_shapes=[pltpu.VMEM((2, MOE_CAP, d), F32), pltpu.SemaphoreType.DMA((2,))],
        compiler_params=_cparams(1),
        name="moe_combine",
    )(info, info, lp_cols, x2_flat, mod, route_flat, ln_g.reshape(1, d), ln_b.reshape(1, d), ys)


def _moe(x2, mod, route, route_t, wegu_bf, wed_bf, ln_g, ln_b, alpha):
    b, s, d = x2.shape
    t = b * s
    tb, tm = MOE_BLOCK, MOE_TILE
    assert s % tb == 0 and TOP_K * tb + N_EXPERTS * (SEG_ALIGN - 1) <= MOE_CAP
    assert tb <= SEG_ALIGN << (SEG_BITS - 1) and SEG_ALIGN == 8
    nb = t // tb
    r = route.reshape(t, LANES)
    eb = route_t.reshape(nb, N_EXPERTS, tb)[:, :TOP_K, :].astype(jnp.int32).reshape(nb, TOP_K * tb)
    na = tb * TOP_K
    onehot = (eb[:, None, :] == jnp.arange(N_EXPERTS, dtype=jnp.int32)[None, :, None])
    tri = (jnp.arange(na)[:, None] <= jnp.arange(na)[None, :]).astype(BF16)
    csum = jnp.dot(onehot.astype(BF16).reshape(nb * N_EXPERTS, na), tri,
                   preferred_element_type=F32).astype(jnp.int32).reshape(nb, N_EXPERTS, na)
    seg_len = ((csum[:, :, -1] + SEG_ALIGN - 1) // SEG_ALIGN) * SEG_ALIGN
    seg_local = jnp.cumsum(seg_len, axis=1) - seg_len
    lp = jnp.sum(jnp.where(onehot, csum - 1 + seg_local[:, :, None], 0), axis=1)
    rows_e = jnp.sum(seg_len, axis=0)
    padded = ((rows_e + tm - 1) // tm) * tm
    ends = jnp.cumsum(padded)
    starts = ends - padded
    seg_glob = starts[None, :] + jnp.cumsum(seg_len, axis=0) - seg_len
    info = jnp.concatenate([seg_local, seg_len, seg_glob], axis=1).astype(jnp.int32).reshape(nb, 1, 3 * N_EXPERTS)
    lp_rows = lp.reshape(nb, TOP_K, tb).astype(jnp.int32)
    lp_cols = lp_rows.transpose(0, 2, 1).reshape(t, TOP_K)
    n_tiles = (t * TOP_K + nb * N_EXPERTS * (SEG_ALIGN - 1) + tm - 1) // tm + N_EXPERTS
    tile_start = jnp.arange(n_tiles, dtype=jnp.int32) * tm
    tile_expert = jnp.minimum(jnp.sum((tile_start[:, None] >= ends[None, :]).astype(jnp.int32), axis=1),
                              N_EXPERTS - 1).astype(jnp.int32)
    n_valid = (ends[-1:] // tm).astype(jnp.int32)
    zfill = jnp.concatenate([jnp.where(padded > 0, ends // tm - 1, -1), n_valid,
                             jnp.full((1,), n_tiles)]).astype(jnp.int32)
    x2_flat = x2.reshape(t, d)
    xs = _dispatch_call(x2_flat, mod, info, zfill, lp_rows, n_tiles * tm, s // tb)
    ys = _expert_call(tile_expert, n_valid, xs, wegu_bf, wed_bf, tm)
    out = _combine_call(x2_flat, mod, r, info, lp_cols, ys, ln_g, ln_b, s // tb, alpha)
    return out.reshape(b, s, d)


def kernel(x_prompt, x_sample, c_prompt, c_sample, w_ada, b_ada, ln_g, ln_b, w_qkv, diff_lambda,
           subln_g, w_o_attn, w_uv, b_uv, ln_v_g, ln_v_b, w_s, b_s, w_o_sgu, w_ffn_gu, w_ffn_down,
           w_router, w_e_gu, w_e_down):
    depth = w_ada.shape[0]
    d = x_prompt.shape[-1]
    alpha = (2 * depth) ** 0.25
    bp = c_prompt.shape[0]
    mods = _ada_call(jnp.concatenate([c_prompt, c_sample], axis=0), w_ada, b_ada)

    w_qkv_bf = w_qkv.astype(BF16)
    w_o_attn_bf = w_o_attn.astype(BF16)
    w_ffn_gu_bf = w_ffn_gu.astype(BF16)
    w_ffn_down_bf = w_ffn_down.astype(BF16)
    w_uv_bf = w_uv.astype(BF16)
    w_s_bf = w_s.astype(BF16)
    w_o_sgu_bf = w_o_sgu.astype(BF16)
    w_e_gu_bf = w_e_gu.astype(BF16)
    w_e_down_bf = w_e_down.astype(BF16)

    def trunk(x, r0, r1):
        for i in range(depth):
            j = i // 2
            mod = mods[i, r0:r1].reshape(r1 - r0, 6, d)
            if i % 2 == 0:
                lambda_init = 0.8 - 0.6 * math.exp(-0.3 * i)
                q2, k, v = _qkv_call(x, mod, w_qkv_bf[j])
                x = _attn_call(q2, k, v, x, mod, diff_lambda[j], subln_g[j], w_o_attn_bf[j],
                               ln_g[i, 0], ln_b[i, 0], lambda_init, alpha)
                x = _ffn_call(x, mod, w_ffn_gu_bf[j], w_ffn_down_bf[j], ln_g[i, 1], ln_b[i, 1], alpha)
            else:
                x, route, route_t = _sgu_call(x, mod, w_uv_bf[j], b_uv[j], ln_v_g[j], ln_v_b[j], w_s_bf[j], b_s[j],
                                              w_o_sgu_bf[j], ln_g[i, 0], ln_b[i, 0], w_router[j], alpha)
                x = _moe(x, mod, route, route_t, w_e_gu_bf[j], w_e_down_bf[j], ln_g[i, 1], ln_b[i, 1], alpha)
        return x

    y_prompt = trunk(x_prompt, 0, bp)
    y_sample = trunk(x_sample, bp, bp + c_sample.shape[0])
    return (y_prompt, y_sample)
```

```python
import functools
import math

import jax
import jax.numpy as jnp
from jax import lax
from jax.experimental import pallas as pl
from jax.experimental.pallas import tpu as pltpu

F32 = jnp.float32
BF16 = jnp.bfloat16

LANES = 128
N_HEADS = 8
HEAD_DIM = 64
V_DIM = 2 * HEAD_DIM
ROPE_THETA = 10000.0
SGU_CHUNK = 128
SGU_GROUPS = 8
N_EXPERTS = 8
TOP_K = 2
LN_EPS = 1e-5
VMEM_LIMIT_BYTES = 56 * 1024 * 1024

HIGHEST = lax.Precision.HIGHEST
LOG2_E = 1.4426950408889634
ATTN_KEY_CHUNK = 256


def _cparams(n_axes):
    return pltpu.CompilerParams(
        dimension_semantics=("arbitrary",) * n_axes,
        vmem_limit_bytes=VMEM_LIMIT_BYTES,
    )


def _ln(y, g, b):
    mu = jnp.mean(y, axis=-1, keepdims=True)
    yc = y - mu
    var = jnp.mean(yc * yc, axis=-1, keepdims=True)
    return yc * lax.rsqrt(var + LN_EPS) * g + b


def _ff_chunks(dff, chunk=512):
    out, c0 = [], 0
    while c0 < dff:
        c1 = min(c0 + chunk, dff)
        out.append((c0, c1))
        c0 = c1
    return out


def _swiglu(h, wgu_at, wd_at, dff):
    f = None
    for c0, c1 in _ff_chunks(dff):
        g = jnp.dot(h, wgu_at(c0, c1), preferred_element_type=F32)
        u = jnp.dot(h, wgu_at(dff + c0, dff + c1), preferred_element_type=F32)
        a = (g * jax.nn.sigmoid(g) * u).astype(BF16)
        p = jnp.dot(a, wd_at(c0, c1), preferred_element_type=F32)
        f = p if f is None else f + p
    return f


def _ada_kernel(c_ref, w_ref, b_ref, o_ref):
    c = c_ref[...]
    s = c * jax.nn.sigmoid(c)
    o_ref[0] = jnp.dot(s, w_ref[0], preferred_element_type=F32, precision=HIGHEST) + b_ref[0]


def _ada_call(c_all, w_ada, b_ada):
    depth, d, n6 = w_ada.shape
    nb = c_all.shape[0]
    tn = 1536
    return pl.pallas_call(
        _ada_kernel,
        grid=(depth, n6 // tn),
        in_specs=[
            pl.BlockSpec((nb, d), lambda l, j: (0, 0)),
            pl.BlockSpec((1, d, tn), lambda l, j: (l, 0, j)),
            pl.BlockSpec((1, 1, tn), lambda l, j: (l, 0, j)),
        ],
        out_specs=pl.BlockSpec((1, nb, tn), lambda l, j: (l, 0, j)),
        out_shape=jax.ShapeDtypeStruct((depth, nb, n6), F32),
        compiler_params=_cparams(2),
        name="ada",
    )(c_all, w_ada, b_ada.reshape(depth, 1, n6))


def _qkv_kernel(x_ref, mod_ref, w_ref, cq_ref, sq_ref, ck_ref, sk_ref, q_ref, k_ref, v_ref):
    d = x_ref.shape[-1]
    ts = x_ref.shape[1]
    x = x_ref[0]
    h = (x * (1.0 + mod_ref[0, 1:2, :]) + mod_ref[0, 0:1, :]).astype(BF16)
    qkv = jnp.dot(h, w_ref[...], preferred_element_type=F32)
    lane = lax.broadcasted_iota(jnp.int32, (ts, LANES), 1)
    first_half = (lane & (HEAD_DIM // 2)) == 0
    is_map1 = lane < HEAD_DIM
    cq, sq, ck, sk = cq_ref[...], sq_ref[...], ck_ref[...], sk_ref[...]

    def rope(t, cos, sin_signed):
        r = jnp.where(first_half, pltpu.roll(t, LANES - HEAD_DIM // 2, 1), pltpu.roll(t, HEAD_DIM // 2, 1))
        return t * cos + r * sin_signed

    for j in range(N_HEADS):
        qr = rope(qkv[:, j * LANES:(j + 1) * LANES], cq, sq)
        q_ref[0, 0, j] = jnp.where(is_map1, qr, 0.0).astype(BF16)
        q_ref[0, 1, j] = jnp.where(is_map1, 0.0, qr).astype(BF16)
        k_ref[0, j] = rope(qkv[:, d + j * LANES:d + (j + 1) * LANES], ck, sk).astype(BF16)
        v_ref[0, j] = qkv[:, 2 * d + j * LANES:2 * d + (j + 1) * LANES].astype(BF16)


def _rope_tables(seq):
    inv = 1.0 / (ROPE_THETA ** (jnp.arange(0, HEAD_DIM, 2, dtype=F32) / HEAD_DIM))
    ang = jnp.arange(seq, dtype=F32)[:, None] * inv[None, :]
    cos = jnp.tile(jnp.cos(ang), (1, 2 * LANES // HEAD_DIM))
    sin = jnp.sin(ang)
    sin_signed = jnp.tile(jnp.concatenate([-sin, sin], axis=-1), (1, LANES // HEAD_DIM))
    scale = HEAD_DIM ** -0.5 * LOG2_E
    return cos * scale, sin_signed * scale, cos, sin_signed


def _qkv_call(x, mod, w_qkv_bf):
    b, s, d = x.shape
    ts = min(512, s)
    cq, sq, ck, sk = _rope_tables(s)
    tab = pl.BlockSpec((ts, LANES), lambda bi, si: (si, 0))
    return pl.pallas_call(
        _qkv_kernel,
        grid=(b, s // ts),
        in_specs=[
            pl.BlockSpec((1, ts, d), lambda bi, si: (bi, si, 0)),
            pl.BlockSpec((1, 6, d), lambda bi, si: (bi, 0, 0)),
            pl.BlockSpec((d, 3 * d), lambda bi, si: (0, 0)),
            tab, tab, tab, tab,
        ],
        out_specs=[
            pl.BlockSpec((1, 2, N_HEADS, ts, LANES), lambda bi, si: (bi, 0, 0, si, 0)),
            pl.BlockSpec((1, N_HEADS, ts, LANES), lambda bi, si: (bi, 0, si, 0)),
            pl.BlockSpec((1, N_HEADS, ts, LANES), lambda bi, si: (bi, 0, si, 0)),
        ],
        out_shape=[
            jax.ShapeDtypeStruct((b, 2, N_HEADS, s, LANES), BF16),
            jax.ShapeDtypeStruct((b, N_HEADS, s, LANES), BF16),
            jax.ShapeDtypeStruct((b, N_HEADS, s, LANES), BF16),
        ],
        compiler_params=_cparams(2),
        name="qkv_rope",
    )(x, mod, w_qkv_bf, cq, sq, ck, sk)


def _attn_kernel(q_ref, k_ref, v_ref, x_ref, mod_ref, lam_ref, sg_ref, wo_ref, lng_ref, lnb_ref,
                 o_ref, s_buf0, s_buf1, e_buf0, e_buf1, *, lambda_init, alpha):
    s_buf = (s_buf0, s_buf1)
    e_buf = (e_buf0, e_buf1)
    tq = x_ref.shape[1]
    ck = ATTN_KEY_CHUNK
    n_chunks = k_ref.shape[2] // ck
    lam = lam_ref[...]
    lam_full = (jnp.exp(jnp.sum(lam[0:1] * lam[1:2], keepdims=True))
                - jnp.exp(jnp.sum(lam[2:3] * lam[3:4], keepdims=True)) + lambda_init)

    def fold_lanes(t, op):
        out = t[:, :LANES]
        for c in range(1, ck // LANES):
            out = op(out, t[:, c * LANES:(c + 1) * LANES])
        return out

    qs, mpart, lpart, acc, m_row, r_row, inv_row = {}, {}, {}, {}, {}, {}, {}
    heads = []
    for i in range(-1, N_HEADS + 1):
        ha, hb, hc = i + 1, i, i - 1
        if 0 <= hb < N_HEADS:
            m_row[hb] = jnp.max(mpart.pop(hb), axis=-1, keepdims=True)
        if 0 <= hc < N_HEADS:
            l = jnp.sum(lpart.pop(hc), axis=-1, keepdims=True)
            r_row[hc] = lam_full * l[:tq] / l[tq:]
            inv_row[hc] = 1.0 / l[:tq]
        if ha < N_HEADS:
            qs[ha] = q_ref[0, :, ha].reshape(2 * tq, LANES)
        for j in range(n_chunks):
            cols = slice(j * ck, (j + 1) * ck)
            if ha < N_HEADS:
                s = lax.dot_general(qs[ha], k_ref[0, ha, cols, :], (((1,), (1,)), ((), ())),
                                    preferred_element_type=F32)
                s_buf[ha % 2][:, cols] = s
                sm = fold_lanes(s, jnp.maximum)
                mpart[ha] = sm if j == 0 else jnp.maximum(mpart[ha], sm)
            if 0 <= hb < N_HEADS:
                e = jnp.exp2(s_buf[hb % 2][:, cols] - m_row[hb])
                e_buf[hb % 2][:, cols] = e
                es = fold_lanes(e, jnp.add)
                lpart[hb] = es if j == 0 else lpart[hb] + es
            if 0 <= hc < N_HEADS:
                a = (e_buf[hc % 2][:tq, cols] - e_buf[hc % 2][tq:, cols] * r_row[hc]).astype(BF16)
                p = jnp.dot(a, v_ref[0, hc, cols, :], preferred_element_type=F32)
                acc[hc] = p if j == 0 else acc[hc] + p
        if 0 <= hc < N_HEADS:
            oh = acc.pop(hc) * inv_row[hc]
            ms = jnp.mean(oh * oh, axis=-1, keepdims=True)
            oh = oh * lax.rsqrt(ms + LN_EPS) * sg_ref[...] * (1.0 - lambda_init)
            heads.append(oh.astype(BF16))
    o = jnp.concatenate(heads, axis=1)
    mix = jnp.dot(o, wo_ref[...], preferred_element_type=F32)
    y = alpha * x_ref[0] + (1.0 + mod_ref[0, 2:3, :]) * mix
    o_ref[0] = _ln(y, lng_ref[...], lnb_ref[...])


def _attn_call(q2, k, v, x, mod, lam, subln_g, wo_bf, ln_g, ln_b, lambda_init, alpha):
    b, s, d = x.shape
    tq = min(256, s)
    const2 = lambda bi, qi: (0, 0)
    return pl.pallas_call(
        functools.partial(_attn_kernel, lambda_init=lambda_init, alpha=alpha),
        grid=(b, s // tq),
        in_specs=[
            pl.BlockSpec((1, 2, N_HEADS, tq, LANES), lambda bi, qi: (bi, 0, 0, qi, 0)),
            pl.BlockSpec((1, N_HEADS, s, LANES), lambda bi, qi: (bi, 0, 0, 0)),
            pl.BlockSpec((1, N_HEADS, s, LANES), lambda bi, qi: (bi, 0, 0, 0)),
            pl.BlockSpec((1, tq, d), lambda bi, qi: (bi, qi, 0)),
            pl.BlockSpec((1, 6, d), lambda bi, qi: (bi, 0, 0)),
            pl.BlockSpec((4, HEAD_DIM), const2),
            pl.BlockSpec((1, V_DIM), const2),
            pl.BlockSpec((d, d), const2),
            pl.BlockSpec((1, d), const2),
            pl.BlockSpec((1, d), const2),
        ],
        out_specs=pl.BlockSpec((1, tq, d), lambda bi, qi: (bi, qi, 0)),
        out_shape=jax.ShapeDtypeStruct((b, s, d), F32),
        scratch_shapes=[pltpu.VMEM((2 * tq, s), F32)] * 4,
        compiler_params=_cparams(2),
        name="diff_attn",
    )(q2, k, v, x, mod, lam, subln_g.reshape(1, V_DIM), wo_bf, ln_g.reshape(1, d), ln_b.reshape(1, d))


def _ffn_kernel(x_ref, mod_ref, wgu_ref, wd_ref, lng_ref, lnb_ref, o_ref, *, alpha):
    dff = wd_ref.shape[0]
    x = x_ref[0]
    h = (x * (1.0 + mod_ref[0, 4:5, :]) + mod_ref[0, 3:4, :]).astype(BF16)
    f = _swiglu(h, lambda c0, c1: wgu_ref[:, c0:c1], lambda c0, c1: wd_ref[c0:c1, :], dff)
    y = alpha * x + (1.0 + mod_ref[0, 5:6, :]) * f
    o_ref[0] = _ln(y, lng_ref[...], lnb_ref[...])


def _ffn_call(x, mod, wgu_bf, wd_bf, ln_g, ln_b, alpha):
    b, s, d = x.shape
    dff = wd_bf.shape[0]
    tm = min(512, s)
    const2 = lambda bi, si: (0, 0)
    return pl.pallas_call(
        functools.partial(_ffn_kernel, alpha=alpha),
        grid=(b, s // tm),
        in_specs=[
            pl.BlockSpec((1, tm, d), lambda bi, si: (bi, si, 0)),
            pl.BlockSpec((1, 6, d), lambda bi, si: (bi, 0, 0)),
            pl.BlockSpec((d, 2 * dff), const2, pipeline_mode=pl.Buffered(1)),
            pl.BlockSpec((dff, d), const2, pipeline_mode=pl.Buffered(1)),
            pl.BlockSpec((1, d), const2),
            pl.BlockSpec((1, d), const2),
        ],
        out_specs=pl.BlockSpec((1, tm, d), lambda bi, si: (bi, si, 0)),
        out_shape=jax.ShapeDtypeStruct((b, s, d), F32),
        compiler_params=_cparams(2),
        name="dense_swiglu",
    )(x, mod, wgu_bf, wd_bf, ln_g.reshape(1, d), ln_b.reshape(1, d))


def _sgu_kernel(x_ref, mod_ref, wuv_ref, buv_ref, lvg_ref, lvb_ref, ws_ref, bs_ref, wo_ref,
                lng_ref, lnb_ref, wr_ref, x2_ref, route_ref, route_t_ref, *, alpha, n_sub):
    tm = x_ref.shape[1]
    sub = tm // n_sub
    width = wo_ref.shape[0]
    blocks = [(i * sub, (i + 1) * sub) for i in range(n_sub)]
    xs = [x_ref[0, r0:r1, :] for r0, r1 in blocks]
    hs = [(x * (1.0 + mod_ref[0, 1:2, :]) + mod_ref[0, 0:1, :]).astype(BF16) for x in xs]
    zs = [jnp.dot(h, wuv_ref[...], preferred_element_type=F32) + buv_ref[...] for h in hs]
    zs = [0.5 * z * (1.0 + lax.erf(z * (2.0 ** -0.5))) for z in zs]
    us = [z[:, :width] for z in zs]
    vs = [_ln(z[:, width:], lvg_ref[...], lvb_ref[...]).astype(BF16) for z in zs]

    def spatial(u, v):
        rows = []
        for n in range(sub // SGU_CHUNK):
            r0, r1 = n * SGU_CHUNK, (n + 1) * SGU_CHUNK
            cols = []
            for g in range(SGU_GROUPS):
                c0, c1 = g * LANES, (g + 1) * LANES
                sp = jnp.dot(ws_ref[g], v[r0:r1, c0:c1], preferred_element_type=F32) + bs_ref[g]
                cols.append(u[r0:r1, c0:c1] * sp)
            rows.append(jnp.concatenate(cols, axis=1))
        return jnp.concatenate(rows, axis=0).astype(BF16)

    gated = [spatial(u, v) for u, v in zip(us, vs)]
    mixes = [jnp.dot(g, wo_ref[...], preferred_element_type=F32) for g in gated]
    x2s = [_ln(alpha * x + (1.0 + mod_ref[0, 2:3, :]) * mix, lng_ref[...], lnb_ref[...])
           for x, mix in zip(xs, mixes)]
    for (r0, r1), x2 in zip(blocks, x2s):
        x2_ref[0, r0:r1, :] = x2

    def split_hi_lo(x2):
        h2 = x2 * (1.0 + mod_ref[0, 4:5, :]) + mod_ref[0, 3:4, :]
        h_hi = h2.astype(BF16)
        h_lo = (h2 - h_hi.astype(F32)).astype(BF16)
        return jnp.concatenate([h_hi, h_lo], axis=0)

    prods = [jnp.dot(split_hi_lo(x2), wr_ref[...], preferred_element_type=F32) for x2 in x2s]
    eidx = lax.broadcasted_iota(jnp.int32, (N_EXPERTS, sub), 0)
    neg = jnp.float32(-jnp.inf)
    for blk, ((r0, r1), p) in enumerate(zip(blocks, prods)):
        both = p[:sub] + p[sub:]
        logits = both + pltpu.roll(both, LANES - N_EXPERTS, 1)
        lt = jnp.transpose(logits)[0:N_EXPERTS, :]
        m1 = jnp.max(lt, axis=0, keepdims=True)
        i1 = jnp.min(jnp.where(lt == m1, eidx, N_EXPERTS), axis=0, keepdims=True)
        rest = jnp.where(eidx == i1, neg, lt)
        m2 = jnp.max(rest, axis=0, keepdims=True)
        i2 = jnp.min(jnp.where(rest == m2, eidx, N_EXPERTS), axis=0, keepdims=True)
        t = jnp.exp(m2 - m1)
        w1 = 1.0 / (1.0 + t)
        w2 = t * w1
        rt = jnp.where(eidx == 0, i1.astype(F32),
                       jnp.where(eidx == 1, i2.astype(F32), jnp.where(eidx == 2, w1, jnp.where(eidx == 3, w2, 0.0))))
        route_t_ref[0, blk] = rt
        padded = jnp.concatenate([rt, jnp.zeros((LANES - N_EXPERTS, sub), F32)], axis=0)
        route_ref[0, r0:r1, :] = jnp.transpose(padded)


def _sgu_call(x, mod, wuv_bf, b_uv, ln_v_g, ln_v_b, ws_bf, b_s, wo_bf, ln_g, ln_b, w_router, alpha):
    b, s, d = x.shape
    width = wo_bf.shape[0]
    tm = min(1024, s)
    assert tm % MOE_BLOCK == 0, "each row block's routing rows feed one dispatch / combine block"
    n_sub = tm // MOE_BLOCK
    w_hi = w_router.astype(BF16)
    w_lo = (w_router - w_hi.astype(F32)).astype(BF16)
    wr_pad = (jnp.zeros((d, LANES), BF16).at[:, :N_EXPERTS].set(w_hi)
              .at[:, N_EXPERTS:2 * N_EXPERTS].set(w_lo))
    const2 = lambda bi, si: (0, 0)
    const3 = lambda bi, si: (0, 0, 0)
    return pl.pallas_call(
        functools.partial(_sgu_kernel, alpha=alpha, n_sub=n_sub),
        grid=(b, s // tm),
        in_specs=[
            pl.BlockSpec((1, tm, d), lambda bi, si: (bi, si, 0)),
            pl.BlockSpec((1, 6, d), lambda bi, si: (bi, 0, 0)),
            pl.BlockSpec((d, 2 * width), const2),
            pl.BlockSpec((1, 2 * width), const2),
            pl.BlockSpec((1, width), const2),
            pl.BlockSpec((1, width), const2),
            pl.BlockSpec((SGU_GROUPS, SGU_CHUNK, SGU_CHUNK), const3),
            pl.BlockSpec((SGU_GROUPS, SGU_CHUNK, 1), const3),
            pl.BlockSpec((width, d), const2),
            pl.BlockSpec((1, d), const2),
            pl.BlockSpec((1, d), const2),
            pl.BlockSpec((d, LANES), const2),
        ],
        out_specs=[
            pl.BlockSpec((1, tm, d), lambda bi, si: (bi, si, 0)),
            pl.BlockSpec((1, tm, LANES), lambda bi, si: (bi, si, 0)),
            pl.BlockSpec((1, n_sub, N_EXPERTS, MOE_BLOCK), lambda bi, si: (bi, si, 0, 0)),
        ],
        out_shape=[
            jax.ShapeDtypeStruct((b, s, d), F32),
            jax.ShapeDtypeStruct((b, s, LANES), F32),
            jax.ShapeDtypeStruct((b, s // MOE_BLOCK, N_EXPERTS, MOE_BLOCK), F32),
        ],
        compiler_params=_cparams(2),
        name="sgu_router",
    )(x, mod, wuv_bf, b_uv.reshape(1, -1), ln_v_g.reshape(1, -1), ln_v_b.reshape(1, -1), ws_bf,
      b_s.reshape(SGU_GROUPS, SGU_CHUNK, 1), wo_bf, ln_g.reshape(1, d), ln_b.reshape(1, d), wr_pad)


MOE_BLOCK = 256
MOE_TILE = 256
SEG_ALIGN = 8
SEG_BITS = 6
MOE_CAP = 640
DISPATCH_SLOTS = 3


def _segment_copies(info_ref, make_copy, action):
    for e in range(N_EXPERTS):
        local = info_ref[0, 0, e]
        n_units = lax.shift_right_logical(info_ref[0, 0, N_EXPERTS + e], 3)
        glob = info_ref[0, 0, 2 * N_EXPERTS + e]
        off = jnp.int32(0)
        for j in reversed(range(SEG_BITS)):
            rows = SEG_ALIGN << j
            bit = lax.shift_right_logical(n_units, j) & 1

            @pl.when(bit == 1)
            def _():
                action(make_copy(pl.multiple_of(local + off, SEG_ALIGN), pl.multiple_of(glob + off, SEG_ALIGN), rows))

            off = off + bit * rows


def _dispatch_kernel(info_ref, infop1_ref, infop2_ref, zfill_ref, lp_ref, x_ref, mod_ref, xs_ref, sbuf, sem):
    i = pl.program_id(0)
    n = pl.num_programs(0)
    tb, d = x_ref.shape
    slot = i % DISPATCH_SLOTS

    @pl.when(i == 0)
    def _():
        sbuf[1, 0:MOE_TILE, :] = jnp.zeros((MOE_TILE, d), F32)

        def zero_tile(t):
            start = pl.multiple_of(t * MOE_TILE, MOE_TILE)
            return pltpu.make_async_copy(sbuf.at[1, pl.ds(0, MOE_TILE)], xs_ref.at[pl.ds(start, MOE_TILE)],
                                         sem.at[DISPATCH_SLOTS])

        for e in range(N_EXPERTS):
            @pl.when(zfill_ref[e] >= 0)
            def _():
                zero_tile(zfill_ref[e]).start()
        for e in range(N_EXPERTS):
            @pl.when(zfill_ref[e] >= 0)
            def _():
                zero_tile(zfill_ref[e]).wait()

        def spare(t, carry):
            cp = zero_tile(t)
            cp.start()
            cp.wait()
            return carry

        lax.fori_loop(zfill_ref[N_EXPERTS], zfill_ref[N_EXPERTS + 1], spare, 0)

    h2 = (x_ref[...] * (1.0 + mod_ref[0, 4:5, :]) + mod_ref[0, 3:4, :]).astype(BF16)
    row = lax.broadcasted_iota(jnp.int32, (MOE_CAP, tb), 0)
    perm = jnp.logical_or(row == lp_ref[0, 0:1, :], row == lp_ref[0, 1:2, :]).astype(BF16)
    sbuf[slot] = jnp.dot(perm, h2, preferred_element_type=F32)

    def copy_out(s):
        return lambda local, glob, rows: pltpu.make_async_copy(
            sbuf.at[s, pl.ds(local, rows)], xs_ref.at[pl.ds(glob, rows)], sem.at[s])

    _segment_copies(info_ref, copy_out(slot), lambda cp: cp.start())

    @pl.when(i >= 2)
    def _():
        _segment_copies(infop2_ref, copy_out((i + 1) % DISPATCH_SLOTS), lambda cp: cp.wait())

    @pl.when(jnp.logical_and(i == n - 1, i >= 1))
    def _():
        _segment_copies(infop1_ref, copy_out((i + 2) % DISPATCH_SLOTS), lambda cp: cp.wait())

    @pl.when(i == n - 1)
    def _():
        _segment_copies(info_ref, copy_out(slot), lambda cp: cp.wait())


def _dispatch_call(x2_flat, mod, info, zfill, lp_rows, n_rows, blocks_per_batch):
    t, d = x2_flat.shape
    tb = MOE_BLOCK
    n_info = info.shape[-1]
    return pl.pallas_call(
        _dispatch_kernel,
        grid=(t // tb,),
        in_specs=[
            pl.BlockSpec((1, 1, n_info), lambda i: (i, 0, 0), memory_space=pltpu.SMEM),
            pl.BlockSpec((1, 1, n_info), lambda i: (jnp.maximum(i - 1, 0), 0, 0), memory_space=pltpu.SMEM),
            pl.BlockSpec((1, 1, n_info), lambda i: (jnp.maximum(i - 2, 0), 0, 0), memory_space=pltpu.SMEM),
            pl.BlockSpec(memory_space=pltpu.SMEM),
            pl.BlockSpec((1, TOP_K, tb), lambda i: (i, 0, 0)),
            pl.BlockSpec((tb, d), lambda i: (i, 0)),
            pl.BlockSpec((1, 6, d), lambda i: (i // blocks_per_batch, 0, 0)),
        ],
        out_specs=pl.BlockSpec(memory_space=pl.ANY),
        out_shape=jax.ShapeDtypeStruct((n_rows, d), F32),
        scratch_shapes=[pltpu.VMEM((DISPATCH_SLOTS, MOE_CAP, d), F32),
                        pltpu.SemaphoreType.DMA((DISPATCH_SLOTS + 1,))],
        compiler_params=_cparams(1),
        name="moe_dispatch",
    )(info, info, info, zfill, lp_rows, x2_flat, mod)


def _expert_kernel(te_ref, nv_ref, xs_ref, wgu_ref, wd_ref, ys_ref):
    del te_ref
    t = pl.program_id(0)
    dff = wd_ref.shape[1]

    @pl.when(t < nv_ref[0])
    def _():
        h = xs_ref[...].astype(BF16)
        ys_ref[...] = _swiglu(h, lambda c0, c1: wgu_ref[0, :, c0:c1], lambda c0, c1: wd_ref[0, c0:c1, :], dff)

    @pl.when(t >= nv_ref[0])
    def _():
        ys_ref[...] = jnp.zeros_like(ys_ref)


def _expert_call(tile_expert, n_valid, xs, wegu_bf, wed_bf, tm):
    n_rows, d = xs.shape
    dff = wed_bf.shape[1]
    grid_spec = pltpu.PrefetchScalarGridSpec(
        num_scalar_prefetch=2,
        grid=(n_rows // tm,),
        in_specs=[
            pl.BlockSpec((tm, d), lambda t, te, nv: (jnp.minimum(t, nv[0] - 1), 0)),
            pl.BlockSpec((1, d, 2 * dff), lambda t, te, nv: (te[t], 0, 0)),
            pl.BlockSpec((1, dff, d), lambda t, te, nv: (te[t], 0, 0)),
        ],
        out_specs=pl.BlockSpec((tm, d), lambda t, te, nv: (t, 0)),
    )
    return pl.pallas_call(
        _expert_kernel,
        grid_spec=grid_spec,
        out_shape=jax.ShapeDtypeStruct((n_rows, d), F32),
        compiler_params=_cparams(1),
        name="moe_experts",
    )(tile_expert, n_valid, xs, wegu_bf, wed_bf)


def _combine_kernel(info_ref, infon_ref, lp_ref, x_ref, mod_ref, route_ref, lng_ref, lnb_ref, ys_ref, o_ref,
                    ybuf, sem, *, alpha):
    i = pl.program_id(0)
    n = pl.num_programs(0)
    tb = x_ref.shape[0]
    slot = i % 2

    def copy_in(s):
        return lambda local, glob, rows: pltpu.make_async_copy(
            ys_ref.at[pl.ds(glob, rows)], ybuf.at[s, pl.ds(local, rows)], sem.at[s])

    @pl.when(i == 0)
    def _():
        ybuf[...] = jnp.zeros(ybuf.shape, F32)
        _segment_copies(info_ref, copy_in(0), lambda cp: cp.start())

    @pl.when(i + 1 < n)
    def _():
        _segment_copies(infon_ref, copy_in(1 - slot), lambda cp: cp.start())

    _segment_copies(info_ref, copy_in(slot), lambda cp: cp.wait())

    yb = ybuf[slot].astype(BF16)
    lane = lax.broadcasted_iota(jnp.int32, (tb, MOE_CAP), 1)
    pick = jnp.concatenate([lane == lp_ref[:, 0:1], lane == lp_ref[:, 1:2]], axis=0).astype(BF16)
    ysel = jnp.dot(pick, yb, preferred_element_type=F32)
    x = x_ref[...]
    f = route_ref[:, 2:3] * ysel[:tb] + route_ref[:, 3:4] * ysel[tb:]
    y = alpha * x + (1.0 + mod_ref[0, 5:6, :]) * f
    o_ref[...] = _ln(y, lng_ref[...], lnb_ref[...])


def _combine_call(x2_flat, mod, route_flat, info, lp_cols, ys, ln_g, ln_b, blocks_per_batch, alpha):
    t, d = x2_flat.shape
    tb = MOE_BLOCK
    n = t // tb
    n_info = info.shape[-1]
    return pl.pallas_call(
        functools.partial(_combine_kernel, alpha=alpha),
        grid=(n,),
        in_specs=[
            pl.BlockSpec((1, 1, n_info), lambda i: (i, 0, 0), memory_space=pltpu.SMEM),
            pl.BlockSpec((1, 1, n_info), lambda i: (jnp.minimum(i + 1, n - 1), 0, 0), memory_space=pltpu.SMEM),
            pl.BlockSpec((tb, TOP_K), lambda i: (i, 0)),
            pl.BlockSpec((tb, d), lambda i: (i, 0)),
            pl.BlockSpec((1, 6, d), lambda i: (i // blocks_per_batch, 0, 0)),
            pl.BlockSpec((tb, LANES), lambda i: (i, 0)),
            pl.BlockSpec((1, d), lambda i: (0, 0)),
            pl.BlockSpec((1, d), lambda i: (0, 0)),
            pl.BlockSpec(memory_space=pl.ANY),
        ],
        out_specs=pl.BlockSpec((tb, d), lambda i: (i, 0)),
        out_shape=jax.ShapeDtypeStruct((t, d), F32),
        scratch_shapes=[pltpu.VMEM((2, MOE_CAP, d), F32), pltpu.SemaphoreType.DMA((2,))],
        compiler_params=_cparams(1),
        name="moe_combine",
    )(info, info, lp_cols, x2_flat, mod, route_flat, ln_g.reshape(1, d), ln_b.reshape(1, d), ys)


def _moe(x2, mod, route, route_t, wegu_bf, wed_bf, ln_g, ln_b, alpha):
    b, s, d = x2.shape
    t = b * s
    tb, tm = MOE_BLOCK, MOE_TILE
    assert s % tb == 0 and TOP_K * tb + N_EXPERTS * (SEG_ALIGN - 1) <= MOE_CAP
    assert tb <= SEG_ALIGN << (SEG_BITS - 1) and SEG_ALIGN == 8
    nb = t // tb
    r = route.reshape(t, LANES)
    eb = route_t.reshape(nb, N_EXPERTS, tb)[:, :TOP_K, :].astype(jnp.int32).reshape(nb, TOP_K * tb)
    na = tb * TOP_K
    onehot = (eb[:, None, :] == jnp.arange(N_EXPERTS, dtype=jnp.int32)[None, :, None])
    tri = (jnp.arange(na)[:, None] <= jnp.arange(na)[None, :]).astype(BF16)
    csum = jnp.dot(onehot.astype(BF16).reshape(nb * N_EXPERTS, na), tri,
                   preferred_element_type=F32).astype(jnp.int32).reshape(nb, N_EXPERTS, na)
    seg_len = ((csum[:, :, -1] + SEG_ALIGN - 1) // SEG_ALIGN) * SEG_ALIGN
    seg_local = jnp.cumsum(seg_len, axis=1) - seg_len
    lp = jnp.sum(jnp.where(onehot, csum - 1 + seg_local[:, :, None], 0), axis=1)
    rows_e = jnp.sum(seg_len, axis=0)
    padded = ((rows_e + tm - 1) // tm) * tm
    ends = jnp.cumsum(padded)
    starts = ends - padded
    seg_glob = starts[None, :] + jnp.cumsum(seg_len, axis=0) - seg_len
    info = jnp.concatenate([seg_local, seg_len, seg_glob], axis=1).astype(jnp.int32).reshape(nb, 1, 3 * N_EXPERTS)
    lp_rows = lp.reshape(nb, TOP_K, tb).astype(jnp.int32)
    lp_cols = lp_rows.transpose(0, 2, 1).reshape(t, TOP_K)
    n_tiles = (t * TOP_K + nb * N_EXPERTS * (SEG_ALIGN - 1) + tm - 1) // tm + N_EXPERTS
    tile_start = jnp.arange(n_tiles, dtype=jnp.int32) * tm
    tile_expert = jnp.minimum(jnp.sum((tile_start[:, None] >= ends[None, :]).astype(jnp.int32), axis=1),
                              N_EXPERTS - 1).astype(jnp.int32)
    n_valid = (ends[-1:] // tm).astype(jnp.int32)
    zfill = jnp.concatenate([jnp.where(padded > 0, ends // tm - 1, -1), n_valid,
                             jnp.full((1,), n_tiles)]).astype(jnp.int32)
    x2_flat = x2.reshape(t, d)
    xs = _dispatch_call(x2_flat, mod, info, zfill, lp_rows, n_tiles * tm, s // tb)
    ys = _expert_call(tile_expert, n_valid, xs, wegu_bf, wed_bf, tm)
    out = _combine_call(x2_flat, mod, r, info, lp_cols, ys, ln_g, ln_b, s // tb, alpha)
    return out.reshape(b, s, d)


def kernel(x_prompt, x_sample, c_prompt, c_sample, w_ada, b_ada, ln_g, ln_b, w_qkv, diff_lambda,
           subln_g, w_o_attn, w_uv, b_uv, ln_v_g, ln_v_b, w_s, b_s, w_o_sgu, w_ffn_gu, w_ffn_down,
           w_router, w_e_gu, w_e_down):
    depth = w_ada.shape[0]
    d = x_prompt.shape[-1]
    alpha = (2 * depth) ** 0.25
    bp = c_prompt.shape[0]
    mods = _ada_call(jnp.concatenate([c_prompt, c_sample], axis=0), w_ada, b_ada)

    w_qkv_bf = w_qkv.astype(BF16)
    w_o_attn_bf = w_o_attn.astype(BF16)
    w_ffn_gu_bf = w_ffn_gu.astype(BF16)
    w_ffn_down_bf = w_ffn_down.astype(BF16)
    w_uv_bf = w_uv.astype(BF16)
    w_s_bf = w_s.astype(BF16)
    w_o_sgu_bf = w_o_sgu.astype(BF16)
    w_e_gu_bf = w_e_gu.astype(BF16)
    w_e_down_bf = w_e_down.astype(BF16)

    def trunk(x, r0, r1):
        for i in range(depth):
            j = i // 2
            mod = mods[i, r0:r1].reshape(r1 - r0, 6, d)
            if i % 2 == 0:
                lambda_init = 0.8 - 0.6 * math.exp(-0.3 * i)
                q2, k, v = _qkv_call(x, mod, w_qkv_bf[j])
                x = _attn_call(q2, k, v, x, mod, diff_lambda[j], subln_g[j], w_o_attn_bf[j],
                               ln_g[i, 0], ln_b[i, 0], lambda_init, alpha)
                x = _ffn_call(x, mod, w_ffn_gu_bf[j], w_ffn_down_bf[j], ln_g[i, 1], ln_b[i, 1], alpha)
            else:
                x, route, route_t = _sgu_call(x, mod, w_uv_bf[j], b_uv[j], ln_v_g[j], ln_v_b[j], w_s_bf[j], b_s[j],
                                              w_o_sgu_bf[j], ln_g[i, 0], ln_b[i, 0], w_router[j], alpha)
                x = _moe(x, mod, route, route_t, w_e_gu_bf[j], w_e_down_bf[j], ln_g[i, 1], ln_b[i, 1], alpha)
        return x

    y_prompt = trunk(x_prompt, 0, bp)
    y_sample = trunk(x_sample, bp, bp + c_sample.shape[0])
    return (y_prompt, y_sample)
```

```python
import functools
import math

import jax
import jax.numpy as jnp
from jax import lax
from jax.experimental import pallas as pl
from jax.experimental.pallas import tpu as pltpu

F32 = jnp.float32
BF16 = jnp.bfloat16

LANES = 128
N_HEADS = 8
HEAD_DIM = 64
V_DIM = 2 * HEAD_DIM
ROPE_THETA = 10000.0
SGU_CHUNK = 128
SGU_GROUPS = 8
N_EXPERTS = 8
TOP_K = 2
LN_EPS = 1e-5
VMEM_LIMIT_BYTES = 56 * 1024 * 1024

HIGHEST = lax.Precision.HIGHEST
LOG2_E = 1.4426950408889634
ATTN_KEY_CHUNK = 256


def _cparams(n_axes):
    return pltpu.CompilerParams(
        dimension_semantics=("arbitrary",) * n_axes,
        vmem_limit_bytes=VMEM_LIMIT_BYTES,
    )


def _ln(y, g, b):
    mu = jnp.mean(y, axis=-1, keepdims=True)
    yc = y - mu
    var = jnp.mean(yc * yc, axis=-1, keepdims=True)
    return yc * lax.rsqrt(var + LN_EPS) * g + b


def _ff_chunks(dff, chunk=512):
    out, c0 = [], 0
    while c0 < dff:
        c1 = min(c0 + chunk, dff)
        out.append((c0, c1))
        c0 = c1
    return out


def _swiglu(h, wgu_at, wd_at, dff):
    f = None
    for c0, c1 in _ff_chunks(dff):
        g = jnp.dot(h, wgu_at(c0, c1), preferred_element_type=F32)
        u = jnp.dot(h, wgu_at(dff + c0, dff + c1), preferred_element_type=F32)
        a = (g * jax.nn.sigmoid(g) * u).astype(BF16)
        p = jnp.dot(a, wd_at(c0, c1), preferred_element_type=F32)
        f = p if f is None else f + p
    return f


def _ada_kernel(c_ref, w_ref, b_ref, o_ref):
    c = c_ref[...]
    s = c * jax.nn.sigmoid(c)
    o_ref[0] = jnp.dot(s, w_ref[0], preferred_element_type=F32, precision=HIGHEST) + b_ref[0]


def _ada_call(c_all, w_ada, b_ada):
    depth, d, n6 = w_ada.shape
    nb = c_all.shape[0]
    tn = 1536
    return pl.pallas_call(
        _ada_kernel,
        grid=(depth, n6 // tn),
        in_specs=[
            pl.BlockSpec((nb, d), lambda l, j: (0, 0)),
            pl.BlockSpec((1, d, tn), lambda l, j: (l, 0, j)),
            pl.BlockSpec((1, 1, tn), lambda l, j: (l, 0, j)),
        ],
        out_specs=pl.BlockSpec((1, nb, tn), lambda l, j: (l, 0, j)),
        out_shape=jax.ShapeDtypeStruct((depth, nb, n6), F32),
        compiler_params=_cparams(2),
        name="ada",
    )(c_all, w_ada, b_ada.reshape(depth, 1, n6))


def _qkv_kernel(x_ref, mod_ref, w_ref, cq_ref, sq_ref, ck_ref, sk_ref, q_ref, k_ref, v_ref):
    d = x_ref.shape[-1]
    ts = x_ref.shape[1]
    x = x_ref[0]
    h = (x * (1.0 + mod_ref[0, 1:2, :]) + mod_ref[0, 0:1, :]).astype(BF16)
    qkv = jnp.dot(h, w_ref[...], preferred_element_type=F32)
    lane = lax.broadcasted_iota(jnp.int32, (ts, LANES), 1)
    first_half = (lane & (HEAD_DIM // 2)) == 0
    is_map1 = lane < HEAD_DIM
    cq, sq, ck, sk = cq_ref[...], sq_ref[...], ck_ref[...], sk_ref[...]

    def rope(t, cos, sin_signed):
        r = jnp.where(first_half, pltpu.roll(t, LANES - HEAD_DIM // 2, 1), pltpu.roll(t, HEAD_DIM // 2, 1))
        return t * cos + r * sin_signed

    for j in range(N_HEADS):
        qr = rope(qkv[:, j * LANES:(j + 1) * LANES], cq, sq)
        q_ref[0, 0, j] = jnp.where(is_map1, qr, 0.0).astype(BF16)
        q_ref[0, 1, j] = jnp.where(is_map1, 0.0, qr).astype(BF16)
        k_ref[0, j] = rope(qkv[:, d + j * LANES:d + (j + 1) * LANES], ck, sk).astype(BF16)
        v_ref[0, j] = qkv[:, 2 * d + j * LANES:2 * d + (j + 1) * LANES].astype(BF16)


def _rope_tables(seq):
    inv = 1.0 / (ROPE_THETA ** (jnp.arange(0, HEAD_DIM, 2, dtype=F32) / HEAD_DIM))
    ang = jnp.arange(seq, dtype=F32)[:, None] * inv[None, :]
    cos = jnp.tile(jnp.cos(ang), (1, 2 * LANES // HEAD_DIM))
    sin = jnp.sin(ang)
    sin_signed = jnp.tile(jnp.concatenate([-sin, sin], axis=-1), (1, LANES // HEAD_DIM))
    scale = HEAD_DIM ** -0.5 * LOG2_E
    return cos * scale, sin_signed * scale, cos, sin_signed


def _qkv_call(x, mod, w_qkv_bf):
    b, s, d = x.shape
    ts = min(512, s)
    cq, sq, ck, sk = _rope_tables(s)
    tab = pl.BlockSpec((ts, LANES), lambda bi, si: (si, 0))
    return pl.pallas_call(
        _qkv_kernel,
        grid=(b, s // ts),
        in_specs=[
            pl.BlockSpec((1, ts, d), lambda bi, si: (bi, si, 0)),
            pl.BlockSpec((1, 6, d), lambda bi, si: (bi, 0, 0)),
            pl.BlockSpec((d, 3 * d), lambda bi, si: (0, 0)),
            tab, tab, tab, tab,
        ],
        out_specs=[
            pl.BlockSpec((1, 2, N_HEADS, ts, LANES), lambda bi, si: (bi, 0, 0, si, 0)),
            pl.BlockSpec((1, N_HEADS, ts, LANES), lambda bi, si: (bi, 0, si, 0)),
            pl.BlockSpec((1, N_HEADS, ts, LANES), lambda bi, si: (bi, 0, si, 0)),
        ],
        out_shape=[
            jax.ShapeDtypeStruct((b, 2, N_HEADS, s, LANES), BF16),
            jax.ShapeDtypeStruct((b, N_HEADS, s, LANES), BF16),
            jax.ShapeDtypeStruct((b, N_HEADS, s, LANES), BF16),
        ],
        compiler_params=_cparams(2),
        name="qkv_rope",
    )(x, mod, w_qkv_bf, cq, sq, ck, sk)


def _attn_kernel(q_ref, k_ref, v_ref, x_ref, mod_ref, lam_ref, sg_ref, wo_ref, lng_ref, lnb_ref,
                 o_ref, s_buf0, s_buf1, e_buf0, e_buf1, *, lambda_init, alpha):
    s_buf = (s_buf0, s_buf1)
    e_buf = (e_buf0, e_buf1)
    tq = x_ref.shape[1]
    ck = ATTN_KEY_CHUNK
    n_chunks = k_ref.shape[2] // ck
    lam = lam_ref[...]
    lam_full = (jnp.exp(jnp.sum(lam[0:1] * lam[1:2], keepdims=True))
                - jnp.exp(jnp.sum(lam[2:3] * lam[3:4], keepdims=True)) + lambda_init)

    def fold_lanes(t, op):
        out = t[:, :LANES]
        for c in range(1, ck // LANES):
            out = op(out, t[:, c * LANES:(c + 1) * LANES])
        return out

    qs, mpart, lpart, acc, m_row, r_row, inv_row = {}, {}, {}, {}, {}, {}, {}
    heads = []
    for i in range(-1, N_HEADS + 1):
        ha, hb, hc = i + 1, i, i - 1
        if 0 <= hb < N_HEADS:
            m_row[hb] = jnp.max(mpart.pop(hb), axis=-1, keepdims=True)
        if 0 <= hc < N_HEADS:
            l = jnp.sum(lpart.pop(hc), axis=-1, keepdims=True)
            r_row[hc] = lam_full * l[:tq] / l[tq:]
            inv_row[hc] = 1.0 / l[:tq]
        if ha < N_HEADS:
            qs[ha] = q_ref[0, :, ha].reshape(2 * tq, LANES)
        for j in range(n_chunks):
            cols = slice(j * ck, (j + 1) * ck)
            if ha < N_HEADS:
                s = lax.dot_general(qs[ha], k_ref[0, ha, cols, :], (((1,), (1,)), ((), ())),
                                    preferred_element_type=F32)
                s_buf[ha % 2][:, cols] = s
                sm = fold_lanes(s, jnp.maximum)
                mpart[ha] = sm if j == 0 else jnp.maximum(mpart[ha], sm)
            if 0 <= hb < N_HEADS:
                e = jnp.exp2(s_buf[hb % 2][:, cols] - m_row[hb])
                e_buf[hb % 2][:, cols] = e
                es = fold_lanes(e, jnp.add)
                lpart[hb] = es if j == 0 else lpart[hb] + es
            if 0 <= hc < N_HEADS:
                a = (e_buf[hc % 2][:tq, cols] - e_buf[hc % 2][tq:, cols] * r_row[hc]).astype(BF16)
                p = jnp.dot(a, v_ref[0, hc, cols, :], preferred_element_type=F32)
                acc[hc] = p if j == 0 else acc[hc] + p
        if 0 <= hc < N_HEADS:
            oh = acc.pop(hc) * inv_row[hc]
            ms = jnp.mean(oh * oh, axis=-1, keepdims=True)
            oh = oh * lax.rsqrt(ms + LN_EPS) * sg_ref[...] * (1.0 - lambda_init)
            heads.append(oh.astype(BF16))
    o = jnp.concatenate(heads, axis=1)
    mix = jnp.dot(o, wo_ref[...], preferred_element_type=F32)
    y = alpha * x_ref[0] + (1.0 + mod_ref[0, 2:3, :]) * mix
    o_ref[0] = _ln(y, lng_ref[...], lnb_ref[...])


def _attn_call(q2, k, v, x, mod, lam, subln_g, wo_bf, ln_g, ln_b, lambda_init, alpha):
    b, s, d = x.shape
    tq = min(256, s)
    const2 = lambda bi, qi: (0, 0)
    return pl.pallas_call(
        functools.partial(_attn_kernel, lambda_init=lambda_init, alpha=alpha),
        grid=(b, s // tq),
        in_specs=[
            pl.BlockSpec((1, 2, N_HEADS, tq, LANES), lambda bi, qi: (bi, 0, 0, qi, 0)),
            pl.BlockSpec((1, N_HEADS, s, LANES), lambda bi, qi: (bi, 0, 0, 0)),
            pl.BlockSpec((1, N_HEADS, s, LANES), lambda bi, qi: (bi, 0, 0, 0)),
            pl.BlockSpec((1, tq, d), lambda bi, qi: (bi, qi, 0)),
            pl.BlockSpec((1, 6, d), lambda bi, qi: (bi, 0, 0)),
            pl.BlockSpec((4, HEAD_DIM), const2),
            pl.BlockSpec((1, V_DIM), const2),
            pl.BlockSpec((d, d), const2),
            pl.BlockSpec((1, d), const2),
            pl.BlockSpec((1, d), const2),
        ],
        out_specs=pl.BlockSpec((1, tq, d), lambda bi, qi: (bi, qi, 0)),
        out_shape=jax.ShapeDtypeStruct((b, s, d), F32),
        scratch_shapes=[pltpu.VMEM((2 * tq, s), F32)] * 4,
        compiler_params=_cparams(2),
        name="diff_attn",
    )(q2, k, v, x, mod, lam, subln_g.reshape(1, V_DIM), wo_bf, ln_g.reshape(1, d), ln_b.reshape(1, d))


def _ffn_kernel(x_ref, mod_ref, wgu_ref, wd_ref, lng_ref, lnb_ref, o_ref, *, alpha):
    dff = wd_ref.shape[0]
    x = x_ref[0]
    h = (x * (1.0 + mod_ref[0, 4:5, :]) + mod_ref[0, 3:4, :]).astype(BF16)
    f = _swiglu(h, lambda c0, c1: wgu_ref[:, c0:c1], lambda c0, c1: wd_ref[c0:c1, :], dff)
    y = alpha * x + (1.0 + mod_ref[0, 5:6, :]) * f
    o_ref[0] = _ln(y, lng_ref[...], lnb_ref[...])


def _ffn_call(x, mod, wgu_bf, wd_bf, ln_g, ln_b, alpha):
    b, s, d = x.shape
    dff = wd_bf.shape[0]
    tm = min(512, s)
    const2 = lambda bi, si: (0, 0)
    return pl.pallas_call(
        functools.partial(_ffn_kernel, alpha=alpha),
        grid=(b, s // tm),
        in_specs=[
            pl.BlockSpec((1, tm, d), lambda bi, si: (bi, si, 0)),
            pl.BlockSpec((1, 6, d), lambda bi, si: (bi, 0, 0)),
            pl.BlockSpec((d, 2 * dff), const2, pipeline_mode=pl.Buffered(1)),
            pl.BlockSpec((dff, d), const2, pipeline_mode=pl.Buffered(1)),
            pl.BlockSpec((1, d), const2),
            pl.BlockSpec((1, d), const2),
        ],
        out_specs=pl.BlockSpec((1, tm, d), lambda bi, si: (bi, si, 0)),
        out_shape=jax.ShapeDtypeStruct((b, s, d), F32),
        compiler_params=_cparams(2),
        name="dense_swiglu",
    )(x, mod, wgu_bf, wd_bf, ln_g.reshape(1, d), ln_b.reshape(1, d))


def _sgu_kernel(x_ref, mod_ref, wuv_ref, buv_ref, lvg_ref, lvb_ref, ws_ref, bs_ref, wo_ref,
                lng_ref, lnb_ref, wr_ref, x2_ref, route_ref, route_t_ref, *, alpha, n_sub):
    tm = x_ref.shape[1]
    sub = tm // n_sub
    width = wo_ref.shape[0]
    blocks = [(i * sub, (i + 1) * sub) for i in range(n_sub)]
    xs = [x_ref[0, r0:r1, :] for r0, r1 in blocks]
    hs = [(x * (1.0 + mod_ref[0, 1:2, :]) + mod_ref[0, 0:1, :]).astype(BF16) for x in xs]
    zs = [jnp.dot(h, wuv_ref[...], preferred_element_type=F32) + buv_ref[...] for h in hs]
    zs = [0.5 * z * (1.0 + lax.erf(z * (2.0 ** -0.5))) for z in zs]
    us = [z[:, :width] for z in zs]
    vs = [_ln(z[:, width:], lvg_ref[...], lvb_ref[...]).astype(BF16) for z in zs]

    def spatial(u, v):
        rows = []
        for n in range(sub // SGU_CHUNK):
            r0, r1 = n * SGU_CHUNK, (n + 1) * SGU_CHUNK
            cols = []
            for g in range(SGU_GROUPS):
                c0, c1 = g * LANES, (g + 1) * LANES
                sp = jnp.dot(ws_ref[g], v[r0:r1, c0:c1], preferred_element_type=F32) + bs_ref[g]
                cols.append(u[r0:r1, c0:c1] * sp)
            rows.append(jnp.concatenate(cols, axis=1))
        return jnp.concatenate(rows, axis=0).astype(BF16)

    gated = [spatial(u, v) for u, v in zip(us, vs)]
    mixes = [jnp.dot(g, wo_ref[...], preferred_element_type=F32) for g in gated]
    x2s = [_ln(alpha * x + (1.0 + mod_ref[0, 2:3, :]) * mix, lng_ref[...], lnb_ref[...])
           for x, mix in zip(xs, mixes)]
    for (r0, r1), x2 in zip(blocks, x2s):
        x2_ref[0, r0:r1, :] = x2

    def split_hi_lo(x2):
        h2 = x2 * (1.0 + mod_ref[0, 4:5, :]) + mod_ref[0, 3:4, :]
        h_hi = h2.astype(BF16)
        h_lo = (h2 - h_hi.astype(F32)).astype(BF16)
        return jnp.concatenate([h_hi, h_lo], axis=0)

    prods = [jnp.dot(split_hi_lo(x2), wr_ref[...], preferred_element_type=F32) for x2 in x2s]
    eidx = lax.broadcasted_iota(jnp.int32, (N_EXPERTS, sub), 0)
    neg = jnp.float32(-jnp.inf)
    for blk, ((r0, r1), p) in enumerate(zip(blocks, prods)):
        both = p[:sub] + p[sub:]
        logits = both + pltpu.roll(both, LANES - N_EXPERTS, 1)
        lt = jnp.transpose(logits)[0:N_EXPERTS, :]
        m1 = jnp.max(lt, axis=0, keepdims=True)
        i1 = jnp.min(jnp.where(lt == m1, eidx, N_EXPERTS), axis=0, keepdims=True)
        rest = jnp.where(eidx == i1, neg, lt)
        m2 = jnp.max(rest, axis=0, keepdims=True)
        i2 = jnp.min(jnp.where(rest == m2, eidx, N_EXPERTS), axis=0, keepdims=True)
        t = jnp.exp(m2 - m1)
        w1 = 1.0 / (1.0 + t)
        w2 = t * w1
        rt = jnp.where(eidx == 0, i1.astype(F32),
                       jnp.where(eidx == 1, i2.astype(F32), jnp.where(eidx == 2, w1, jnp.where(eidx == 3, w2, 0.0))))
        route_t_ref[0, blk] = rt
        padded = jnp.concatenate([rt, jnp.zeros((LANES - N_EXPERTS, sub), F32)], axis=0)
        route_ref[0, r0:r1, :] = jnp.transpose(padded)


def _sgu_call(x, mod, wuv_bf, b_uv, ln_v_g, ln_v_b, ws_bf, b_s, wo_bf, ln_g, ln_b, w_router, alpha):
    b, s, d = x.shape
    width = wo_bf.shape[0]
    tm = min(1024, s)
    assert tm % MOE_BLOCK == 0, "each row block's routing rows feed one dispatch / combine block"
    n_sub = tm // MOE_BLOCK
    w_hi = w_router.astype(BF16)
    w_lo = (w_router - w_hi.astype(F32)).astype(BF16)
    wr_pad = (jnp.zeros((d, LANES), BF16).at[:, :N_EXPERTS].set(w_hi)
              .at[:, N_EXPERTS:2 * N_EXPERTS].set(w_lo))
    const2 = lambda bi, si: (0, 0)
    const3 = lambda bi, si: (0, 0, 0)
    return pl.pallas_call(
        functools.partial(_sgu_kernel, alpha=alpha, n_sub=n_sub),
        grid=(b, s // tm),
        in_specs=[
            pl.BlockSpec((1, tm, d), lambda bi, si: (bi, si, 0)),
            pl.BlockSpec((1, 6, d), lambda bi, si: (bi, 0, 0)),
            pl.BlockSpec((d, 2 * width), const2),
            pl.BlockSpec((1, 2 * width), const2),
            pl.BlockSpec((1, width), const2),
            pl.BlockSpec((1, width), const2),
            pl.BlockSpec((SGU_GROUPS, SGU_CHUNK, SGU_CHUNK), const3),
            pl.BlockSpec((SGU_GROUPS, SGU_CHUNK, 1), const3),
            pl.BlockSpec((width, d), const2),
            pl.BlockSpec((1, d), const2),
            pl.BlockSpec((1, d), const2),
            pl.BlockSpec((d, LANES), const2),
        ],
        out_specs=[
            pl.BlockSpec((1, tm, d), lambda bi, si: (bi, si, 0)),
            pl.BlockSpec((1, tm, LANES), lambda bi, si: (bi, si, 0)),
            pl.BlockSpec((1, n_sub, N_EXPERTS, MOE_BLOCK), lambda bi, si: (bi, si, 0, 0)),
        ],
        out_shape=[
            jax.ShapeDtypeStruct((b, s, d), F32),
            jax.ShapeDtypeStruct((b, s, LANES), F32),
            jax.ShapeDtypeStruct((b, s // MOE_BLOCK, N_EXPERTS, MOE_BLOCK), F32),
        ],
        compiler_params=_cparams(2),
        name="sgu_router",
    )(x, mod, wuv_bf, b_uv.reshape(1, -1), ln_v_g.reshape(1, -1), ln_v_b.reshape(1, -1), ws_bf,
      b_s.reshape(SGU_GROUPS, SGU_CHUNK, 1), wo_bf, ln_g.reshape(1, d), ln_b.reshape(1, d), wr_pad)


MOE_BLOCK = 256
MOE_TILE = 256
SEG_ALIGN = 8
SEG_BITS = 6
MOE_CAP = 640
DISPATCH_SLOTS = 3


def _segment_copies(info_ref, make_copy, action):
    for e in range(N_EXPERTS):
        local = info_ref[0, 0, e]
        n_units = lax.shift_right_logical(info_ref[0, 0, N_EXPERTS + e], 3)
        glob = info_ref[0, 0, 2 * N_EXPERTS + e]
        off = jnp.int32(0)
        for j in reversed(range(SEG_BITS)):
            rows = SEG_ALIGN << j
            bit = lax.shift_right_logical(n_units, j) & 1

            @pl.when(bit == 1)
            def _():
                action(make_copy(pl.multiple_of(local + off, SEG_ALIGN), pl.multiple_of(glob + off, SEG_ALIGN), rows),
                       e % 2)

            off = off + bit * rows


def _dispatch_kernel(info_ref, infop1_ref, infop2_ref, zfill_ref, lp_ref, x_ref, mod_ref, xs_ref, sbuf, sem):
    i = pl.program_id(0)
    n = pl.num_programs(0)
    tb, d = x_ref.shape
    slot = i % DISPATCH_SLOTS

    @pl.when(i == 0)
    def _():
        sbuf[1, 0:MOE_TILE, :] = jnp.zeros((MOE_TILE, d), F32)

        def zero_tile(t):
            start = pl.multiple_of(t * MOE_TILE, MOE_TILE)
            return pltpu.make_async_copy(sbuf.at[1, pl.ds(0, MOE_TILE)], xs_ref.at[pl.ds(start, MOE_TILE)],
                                         sem.at[DISPATCH_SLOTS])

        for e in range(N_EXPERTS):
            @pl.when(zfill_ref[e] >= 0)
            def _():
                zero_tile(zfill_ref[e]).start()
        for e in range(N_EXPERTS):
            @pl.when(zfill_ref[e] >= 0)
            def _():
                zero_tile(zfill_ref[e]).wait()

        def spare(t, carry):
            cp = zero_tile(t)
            cp.start()
            cp.wait()
            return carry

        lax.fori_loop(zfill_ref[N_EXPERTS], zfill_ref[N_EXPERTS + 1], spare, 0)

    h2 = (x_ref[...] * (1.0 + mod_ref[0, 4:5, :]) + mod_ref[0, 3:4, :]).astype(BF16)
    row = lax.broadcasted_iota(jnp.int32, (MOE_CAP, tb), 0)
    perm = jnp.logical_or(row == lp_ref[0, 0:1, :], row == lp_ref[0, 1:2, :]).astype(BF16)
    sbuf[slot] = jnp.dot(perm, h2, preferred_element_type=F32)

    def copy_out(s):
        return lambda local, glob, rows: pltpu.make_async_copy(
            sbuf.at[s, pl.ds(local, rows)], xs_ref.at[pl.ds(glob, rows)], sem.at[s])

    _segment_copies(info_ref, copy_out(slot), lambda cp, prio: cp.start(priority=prio))

    @pl.when(i >= 2)
    def _():
        _segment_copies(infop2_ref, copy_out((i + 1) % DISPATCH_SLOTS), lambda cp, prio: cp.wait())

    @pl.when(jnp.logical_and(i == n - 1, i >= 1))
    def _():
        _segment_copies(infop1_ref, copy_out((i + 2) % DISPATCH_SLOTS), lambda cp, prio: cp.wait())

    @pl.when(i == n - 1)
    def _():
        _segment_copies(info_ref, copy_out(slot), lambda cp, prio: cp.wait())


def _dispatch_call(x2_flat, mod, info, zfill, lp_rows, n_rows, blocks_per_batch):
    t, d = x2_flat.shape
    tb = MOE_BLOCK
    n_info = info.shape[-1]
    return pl.pallas_call(
        _dispatch_kernel,
        grid=(t // tb,),
        in_specs=[
            pl.BlockSpec((1, 1, n_info), lambda i: (i, 0, 0), memory_space=pltpu.SMEM),
            pl.BlockSpec((1, 1, n_info), lambda i: (jnp.maximum(i - 1, 0), 0, 0), memory_space=pltpu.SMEM),
            pl.BlockSpec((1, 1, n_info), lambda i: (jnp.maximum(i - 2, 0), 0, 0), memory_space=pltpu.SMEM),
            pl.BlockSpec(memory_space=pltpu.SMEM),
            pl.BlockSpec((1, TOP_K, tb), lambda i: (i, 0, 0)),
            pl.BlockSpec((tb, d), lambda i: (i, 0)),
            pl.BlockSpec((1, 6, d), lambda i: (i // blocks_per_batch, 0, 0)),
        ],
        out_specs=pl.BlockSpec(memory_space=pl.ANY),
        out_shape=jax.ShapeDtypeStruct((n_rows, d), F32),
        scratch_shapes=[pltpu.VMEM((DISPATCH_SLOTS, MOE_CAP, d), F32),
                        pltpu.SemaphoreType.DMA((DISPATCH_SLOTS + 1,))],
        compiler_params=_cparams(1),
        name="moe_dispatch",
    )(info, info, info, zfill, lp_rows, x2_flat, mod)


def _expert_kernel(te_ref, nv_ref, xs_ref, wgu_ref, wd_ref, ys_ref):
    del te_ref
    t = pl.program_id(0)
    dff = wd_ref.shape[1]

    @pl.when(t < nv_ref[0])
    def _():
        h = xs_ref[...].astype(BF16)
        ys_ref[...] = _swiglu(h, lambda c0, c1: wgu_ref[0, :, c0:c1], lambda c0, c1: wd_ref[0, c0:c1, :], dff)

    @pl.when(t >= nv_ref[0])
    def _():
        ys_ref[...] = jnp.zeros_like(ys_ref)


def _expert_call(tile_expert, n_valid, xs, wegu_bf, wed_bf, tm):
    n_rows, d = xs.shape
    dff = wed_bf.shape[1]
    grid_spec = pltpu.PrefetchScalarGridSpec(
        num_scalar_prefetch=2,
        grid=(n_rows // tm,),
        in_specs=[
            pl.BlockSpec((tm, d), lambda t, te, nv: (jnp.minimum(t, nv[0] - 1), 0)),
            pl.BlockSpec((1, d, 2 * dff), lambda t, te, nv: (te[t], 0, 0)),
            pl.BlockSpec((1, dff, d), lambda t, te, nv: (te[t], 0, 0)),
        ],
        out_specs=pl.BlockSpec((tm, d), lambda t, te, nv: (t, 0)),
    )
    return pl.pallas_call(
        _expert_kernel,
        grid_spec=grid_spec,
        out_shape=jax.ShapeDtypeStruct((n_rows, d), F32),
        compiler_params=_cparams(1),
        name="moe_experts",
    )(tile_expert, n_valid, xs, wegu_bf, wed_bf)


def _combine_kernel(info_ref, infon_ref, lp_ref, x_ref, mod_ref, route_ref, lng_ref, lnb_ref, ys_ref, o_ref,
                    ybuf, sem, *, alpha):
    i = pl.program_id(0)
    n = pl.num_programs(0)
    tb = x_ref.shape[0]
    slot = i % 2

    def copy_in(s):
        return lambda local, glob, rows: pltpu.make_async_copy(
            ys_ref.at[pl.ds(glob, rows)], ybuf.at[s, pl.ds(local, rows)], sem.at[s])

    @pl.when(i == 0)
    def _():
        ybuf[...] = jnp.zeros(ybuf.shape, F32)
        _segment_copies(info_ref, copy_in(0), lambda cp, prio: cp.start(priority=prio))

    @pl.when(i + 1 < n)
    def _():
        _segment_copies(infon_ref, copy_in(1 - slot), lambda cp, prio: cp.start(priority=prio))

    _segment_copies(info_ref, copy_in(slot), lambda cp, prio: cp.wait())

    yb = ybuf[slot].astype(BF16)
    lane = lax.broadcasted_iota(jnp.int32, (tb, MOE_CAP), 1)
    pick = jnp.concatenate([lane == lp_ref[:, 0:1], lane == lp_ref[:, 1:2]], axis=0).astype(BF16)
    ysel = jnp.dot(pick, yb, preferred_element_type=F32)
    x = x_ref[...]
    f = route_ref[:, 2:3] * ysel[:tb] + route_ref[:, 3:4] * ysel[tb:]
    y = alpha * x + (1.0 + mod_ref[0, 5:6, :]) * f
    o_ref[...] = _ln(y, lng_ref[...], lnb_ref[...])


def _combine_call(x2_flat, mod, route_flat, info, lp_cols, ys, ln_g, ln_b, blocks_per_batch, alpha):
    t, d = x2_flat.shape
    tb = MOE_BLOCK
    n = t // tb
    n_info = info.shape[-1]
    return pl.pallas_call(
        functools.partial(_combine_kernel, alpha=alpha),
        grid=(n,),
        in_specs=[
            pl.BlockSpec((1, 1, n_info), lambda i: (i, 0, 0), memory_space=pltpu.SMEM),
            pl.BlockSpec((1, 1, n_info), lambda i: (jnp.minimum(i + 1, n - 1), 0, 0), memory_space=pltpu.SMEM),
            pl.BlockSpec((tb, TOP_K), lambda i: (i, 0)),
            pl.BlockSpec((tb, d), lambda i: (i, 0)),
            pl.BlockSpec((1, 6, d), lambda i: (i // blocks_per_batch, 0, 0)),
            pl.BlockSpec((tb, LANES), lambda i: (i, 0)),
            pl.BlockSpec((1, d), lambda i: (0, 0)),
            pl.BlockSpec((1, d), lambda i: (0, 0)),
            pl.BlockSpec(memory_space=pl.ANY),
        ],
        out_specs=pl.BlockSpec((tb, d), lambda i: (i, 0)),
        out_shape=jax.ShapeDtypeStruct((t, d), F32),
        scratch_shapes=[pltpu.VMEM((2, MOE_CAP, d), F32), pltpu.SemaphoreType.DMA((2,))],
        compiler_params=_cparams(1),
        name="moe_combine",
    )(info, info, lp_cols, x2_flat, mod, route_flat, ln_g.reshape(1, d), ln_b.reshape(1, d), ys)


def _moe(x2, mod, route, route_t, wegu_bf, wed_bf, ln_g, ln_b, alpha):
    b, s, d = x2.shape
    t = b * s
    tb, tm = MOE_BLOCK, MOE_TILE
    assert s % tb == 0 and TOP_K * tb + N_EXPERTS * (SEG_ALIGN - 1) <= MOE_CAP
    assert tb <= SEG_ALIGN << (SEG_BITS - 1) and SEG_ALIGN == 8
    nb = t // tb
    r = route.reshape(t, LANES)
    eb = route_t.reshape(nb, N_EXPERTS, tb)[:, :TOP_K, :].astype(jnp.int32).reshape(nb, TOP_K * tb)
    na = tb * TOP_K
    onehot = (eb[:, None, :] == jnp.arange(N_EXPERTS, dtype=jnp.int32)[None, :, None])
    tri = (jnp.arange(na)[:, None] <= jnp.arange(na)[None, :]).astype(BF16)
    csum = jnp.dot(onehot.astype(BF16).reshape(nb * N_EXPERTS, na), tri,
                   preferred_element_type=F32).astype(jnp.int32).reshape(nb, N_EXPERTS, na)
    seg_len = ((csum[:, :, -1] + SEG_ALIGN - 1) // SEG_ALIGN) * SEG_ALIGN
    seg_local = jnp.cumsum(seg_len, axis=1) - seg_len
    lp = jnp.sum(jnp.where(onehot, csum - 1 + seg_local[:, :, None], 0), axis=1)
    rows_e = jnp.sum(seg_len, axis=0)
    padded = ((rows_e + tm - 1) // tm) * tm
    ends = jnp.cumsum(padded)
    starts = ends - padded
    seg_glob = starts[None, :] + jnp.cumsum(seg_len, axis=0) - seg_len
    info = jnp.concatenate([seg_local, seg_len, seg_glob], axis=1).astype(jnp.int32).reshape(nb, 1, 3 * N_EXPERTS)
    lp_rows = lp.reshape(nb, TOP_K, tb).astype(jnp.int32)
    lp_cols = lp_rows.transpose(0, 2, 1).reshape(t, TOP_K)
    n_tiles = (t * TOP_K + nb * N_EXPERTS * (SEG_ALIGN - 1) + tm - 1) // tm + N_EXPERTS
    tile_start = jnp.arange(n_tiles, dtype=jnp.int32) * tm
    tile_expert = jnp.minimum(jnp.sum((tile_start[:, None] >= ends[None, :]).astype(jnp.int32), axis=1),
                              N_EXPERTS - 1).astype(jnp.int32)
    n_valid = (ends[-1:] // tm).astype(jnp.int32)
    zfill = jnp.concatenate([jnp.where(padded > 0, ends // tm - 1, -1), n_valid,
                             jnp.full((1,), n_tiles)]).astype(jnp.int32)
    x2_flat = x2.reshape(t, d)
    xs = _dispatch_call(x2_flat, mod, info, zfill, lp_rows, n_tiles * tm, s // tb)
    ys = _expert_call(tile_expert, n_valid, xs, wegu_bf, wed_bf, tm)
    out = _combine_call(x2_flat, mod, r, info, lp_cols, ys, ln_g, ln_b, s // tb, alpha)
    return out.reshape(b, s, d)


def kernel(x_prompt, x_sample, c_prompt, c_sample, w_ada, b_ada, ln_g, ln_b, w_qkv, diff_lambda,
           subln_g, w_o_attn, w_uv, b_uv, ln_v_g, ln_v_b, w_s, b_s, w_o_sgu, w_ffn_gu, w_ffn_down,
           w_router, w_e_gu, w_e_down):
    depth = w_ada.shape[0]
    d = x_prompt.shape[-1]
    alpha = (2 * depth) ** 0.25
    bp = c_prompt.shape[0]
    mods = _ada_call(jnp.concatenate([c_prompt, c_sample], axis=0), w_ada, b_ada)

    w_qkv_bf = w_qkv.astype(BF16)
    w_o_attn_bf = w_o_attn.astype(BF16)
    w_ffn_gu_bf = w_ffn_gu.astype(BF16)
    w_ffn_down_bf = w_ffn_down.astype(BF16)
    w_uv_bf = w_uv.astype(BF16)
    w_s_bf = w_s.astype(BF16)
    w_o_sgu_bf = w_o_sgu.astype(BF16)
    w_e_gu_bf = w_e_gu.astype(BF16)
    w_e_down_bf = w_e_down.astype(BF16)

    def trunk(x, r0, r1):
        for i in range(depth):
            j = i // 2
            mod = mods[i, r0:r1].reshape(r1 - r0, 6, d)
            if i % 2 == 0:
                lambda_init = 0.8 - 0.6 * math.exp(-0.3 * i)
                q2, k, v = _qkv_call(x, mod, w_qkv_bf[j])
                x = _attn_call(q2, k, v, x, mod, diff_lambda[j], subln_g[j], w_o_attn_bf[j],
                               ln_g[i, 0], ln_b[i, 0], lambda_init, alpha)
                x = _ffn_call(x, mod, w_ffn_gu_bf[j], w_ffn_down_bf[j], ln_g[i, 1], ln_b[i, 1], alpha)
            else:
                x, route, route_t = _sgu_call(x, mod, w_uv_bf[j], b_uv[j], ln_v_g[j], ln_v_b[j], w_s_bf[j], b_s[j],
                                              w_o_sgu_bf[j], ln_g[i, 0], ln_b[i, 0], w_router[j], alpha)
                x = _moe(x, mod, route, route_t, w_e_gu_bf[j], w_e_down_bf[j], ln_g[i, 1], ln_b[i, 1], alpha)
        return x

    y_prompt = trunk(x_prompt, 0, bp)
    y_sample = trunk(x_sample, bp, bp + c_sample.shape[0])
    return (y_prompt, y_sample)
```
